```python
import math
import jax, jax.numpy as jnp
from jax import lax
import numpy as np

D_MODEL = 1024
BATCH = 4
SEQ = 8192
DEPTH = 1

PLE_DIM = 256
GLA_HEADS = 4
GLA_DK = D_MODEL // 2 // GLA_HEADS
GLA_DV = D_MODEL // GLA_HEADS
GLA_GATE_RANK = 16
GLA_GATE_TEMP = 16.0
GLA_CHUNK = 64
DSA_HEADS = 16
DSA_HEAD_DIM = D_MODEL // DSA_HEADS
DSA_LATENT = 128
IDX_HEADS = 8
IDX_DIM = 64
TOPK_MAX = 256
Q_BLOCK = 128
D_FF = 2816
CONV_W = 3
LN_EPS = 1e-5
DEEPNORM_ALPHA = (2.0 * DEPTH) ** 0.25
DEEPNORM_BETA = (8.0 * DEPTH) ** -0.25
SPLIT_SIZES = (
    GLA_HEADS * GLA_DK,
    GLA_HEADS * GLA_DK,
    GLA_HEADS * GLA_DV,
    GLA_HEADS * GLA_DV,
    GLA_GATE_RANK,
    DSA_HEADS * DSA_LATENT,
    DSA_LATENT,
    IDX_HEADS * IDX_DIM,
    IDX_DIM,
    IDX_HEADS,
    2 * D_MODEL,
)
IN_COLS = sum(SPLIT_SIZES)

kernel_name = 'hybrid_gla_dsa_convffn_block'


def _split_points():
    return [int(v) for v in np.cumsum(np.array(SPLIT_SIZES))[:-1]]


def layer_norm(x, g, b):
    xf = x.astype(jnp.float32)
    mu = jnp.mean(xf, axis=-1, keepdims=True)
    var = jnp.mean(jnp.square(xf - mu), axis=-1, keepdims=True)
    return ((xf - mu) * lax.rsqrt(var + LN_EPS) * g.astype(jnp.float32) + b.astype(jnp.float32)).astype(x.dtype)


def rms_norm(x, g):
    xf = x.astype(jnp.float32)
    ms = jnp.mean(jnp.square(xf), axis=-1, keepdims=True)
    return (xf * lax.rsqrt(ms + LN_EPS) * g.astype(jnp.float32)).astype(x.dtype)


def gla_chunked(q, k, v, log_a):
    q, k, v, log_a = (t.astype(jnp.float32) for t in (q, k, v, log_a))
    B, L, H, dk = q.shape
    dv = v.shape[-1]
    C = GLA_CHUNK
    N = L // C

    def to_chunks(t):
        return t.reshape(B, N, C, H, t.shape[-1]).transpose(1, 0, 3, 2, 4)

    qc, kc, vc, ac = (to_chunks(t) for t in (q, k, v, log_a))
    b = jnp.cumsum(ac, axis=3)
    b_last = b[:, :, :, -1:, :]
    q_in = qc * jnp.exp(b)
    k_in = kc * jnp.exp(-b)
    k_state = kc * jnp.exp(b_last - b)
    causal = jnp.tril(jnp.ones((C, C), dtype=bool))
    attn = jnp.where(causal, jnp.einsum('nbhid,nbhjd->nbhij', q_in, k_in), 0.0)
    o_intra = jnp.einsum('nbhij,nbhjv->nbhiv', attn, vc)

    def step(S, xs):
        q_i, k_i, v_i, decay_i = xs
        o = jnp.einsum('bhid,bhdv->bhiv', q_i, S)
        S = S * decay_i[:, :, 0, :, None] + jnp.einsum('bhjd,bhjv->bhdv', k_i, v_i)
        return S, o

    S0 = jnp.zeros((B, H, dk, dv), jnp.float32)
    _, o_inter = lax.scan(step, S0, (q_in, k_state, vc, jnp.exp(b_last)))
    o = o_intra + o_inter
    return o.transpose(1, 0, 3, 2, 4).reshape(B, L, H, dv)


def dsa_attention(q, ckv, iq, ik, iw):
    B, L, H, dc = q.shape
    top_k = min(TOPK_MAX, L // 4)
    nb = L // Q_BLOCK
    key_pos = jnp.arange(L, dtype=jnp.int32)

    def blockify(t):
        return jnp.swapaxes(t.reshape((B, nb, Q_BLOCK) + t.shape[2:]), 0, 1)

    def one_block(args):
        q_b, iq_b, iw_b, q_pos = args
        s = jax.nn.relu(jnp.einsum('bqhd,bsd->bqhs', iq_b, ik).astype(jnp.float32))
        score = jnp.einsum('bqh,bqhs->bqs', iw_b.astype(jnp.float32), s)
        causal = key_pos[None, :] <= q_pos[:, None]
        score = jnp.where(causal[None], score, -jnp.inf)
        _, idx = lax.top_k(score, top_k)
        valid = idx <= q_pos[None, :, None]
        kv_sel = jax.vmap(lambda c, i: c[i])(ckv, idx)
        logits = jnp.einsum('bqhc,bqkc->bqhk', q_b, kv_sel).astype(jnp.float32) * (dc ** -0.5)
        logits = jnp.where(valid[:, :, None, :], logits, -jnp.inf)
        probs = jax.nn.softmax(logits, axis=-1).astype(kv_sel.dtype)
        return jnp.einsum('bqhk,bqkc->bqhc', probs, kv_sel)

    pos_blocks = jnp.arange(L, dtype=jnp.int32).reshape(nb, Q_BLOCK)
    out = lax.map(one_block, (blockify(q), blockify(iq), blockify(iw), pos_blocks))
    return jnp.swapaxes(out, 0, 1).reshape(B, L, H, dc)


def causal_dwconv(h, w, b):
    L = h.shape[1]
    hp = jnp.pad(h, ((0, 0), (CONV_W - 1, 0), (0, 0)))
    out = w[CONV_W - 1] * h
    for j in range(CONV_W - 1):
        out = out + w[j] * hp[:, j:j + L]
    return out + b


def setup_inputs(seed: int = 0) -> dict:
    key = jax.random.key(seed)
    ks = jax.random.split(key, 24)

    def nrm(k, shape, fan_in, scale=1.0):
        return jax.random.normal(k, shape, jnp.float32) * (scale * fan_in ** -0.5)

    def gain(k, shape):
        return 1.0 + 0.02 * jax.random.normal(k, shape, jnp.float32)

    def bias(k, shape, s=0.02):
        return s * jax.random.normal(k, shape, jnp.float32)

    return {
        'x': jax.random.normal(ks[0], (BATCH, SEQ, D_MODEL), jnp.float32),
        'p': jax.random.normal(ks[1], (DEPTH, BATCH, SEQ, PLE_DIM), jnp.float32),
        'w_in': nrm(ks[2], (DEPTH, D_MODEL, IN_COLS), D_MODEL),
        'w_gla_gate_up': nrm(ks[3], (DEPTH, GLA_GATE_RANK, GLA_HEADS * GLA_DK), GLA_GATE_RANK),
        'b_gla_gate': bias(ks[4], (DEPTH, GLA_HEADS * GLA_DK), 0.1),
        'g_gla_norm': gain(ks[5], (DEPTH, GLA_HEADS * GLA_DV)),
        'w_gla_proj': nrm(ks[6], (DEPTH, GLA_HEADS * GLA_DV, D_MODEL), GLA_HEADS * GLA_DV),
        'g_ckv_norm': gain(ks[7], (DEPTH, DSA_LATENT)),
        'w_uv': nrm(ks[8], (DEPTH, DSA_HEADS, DSA_LATENT, DSA_HEAD_DIM), DSA_LATENT),
        'w_dsa_proj': nrm(ks[9], (DEPTH, DSA_HEADS * DSA_HEAD_DIM, D_MODEL), DSA_HEADS * DSA_HEAD_DIM),
        'w_out': nrm(ks[10], (DEPTH, D_MODEL, D_MODEL), D_MODEL, DEEPNORM_BETA),
        'ln1_g': gain(ks[11], (DEPTH, D_MODEL)),
        'ln1_b': bias(ks[12], (DEPTH, D_MODEL)),
        'w_up': nrm(ks[13], (DEPTH, D_MODEL, 2 * D_FF), D_MODEL),
        'conv_w': nrm(ks[14], (DEPTH, CONV_W, 2 * D_FF), CONV_W),
        'conv_b': bias(ks[15], (DEPTH, 2 * D_FF)),
        'w_down': nrm(ks[16], (DEPTH, D_FF, D_MODEL), D_FF, DEEPNORM_BETA),
        'ln2_g': gain(ks[17], (DEPTH, D_MODEL)),
        'ln2_b': bias(ks[18], (DEPTH, D_MODEL)),
        'w_ple': nrm(ks[19], (DEPTH, PLE_DIM, D_MODEL), PLE_DIM, DEEPNORM_BETA),
        'w_ple_gate': nrm(ks[20], (DEPTH, D_MODEL, D_MODEL), D_MODEL),
        'ln3_g': gain(ks[21], (DEPTH, D_MODEL)),
        'ln3_b': bias(ks[22], (DEPTH, D_MODEL)),
    }


def reference(x, p, w_in, w_gla_gate_up, b_gla_gate, g_gla_norm, w_gla_proj, g_ckv_norm, w_uv,
              w_dsa_proj, w_out, ln1_g, ln1_b, w_up, conv_w, conv_b, w_down, ln2_g, ln2_b,
              w_ple, w_ple_gate, ln3_g, ln3_b):
    B, L, _ = x.shape
    for i in range(DEPTH):
        proj = x @ w_in[i]
        gq, gk, gv, gr, ga, dq, ckv, iq, ik, iw, gates = jnp.split(proj, _split_points(), axis=-1)

        q_a = gq.reshape(B, L, GLA_HEADS, GLA_DK) * (GLA_DK ** -0.5)
        k_a = gk.reshape(B, L, GLA_HEADS, GLA_DK)
        v_a = gv.reshape(B, L, GLA_HEADS, GLA_DV)
        z = (ga @ w_gla_gate_up[i] + b_gla_gate[i]).astype(jnp.float32)
        log_a = (jax.nn.log_sigmoid(z) / GLA_GATE_TEMP).reshape(B, L, GLA_HEADS, GLA_DK)
        o_a = gla_chunked(q_a, k_a, v_a, log_a)
        o_a = rms_norm(o_a, g_gla_norm[i].reshape(GLA_HEADS, GLA_DV))
        o_a = o_a.reshape(B, L, GLA_HEADS * GLA_DV).astype(x.dtype) * jax.nn.silu(gr)
        y_a = o_a @ w_gla_proj[i]

        q_b = dq.reshape(B, L, DSA_HEADS, DSA_LATENT)
        c_kv = rms_norm(ckv, g_ckv_norm[i])
        iq_b = iq.reshape(B, L, IDX_HEADS, IDX_DIM)
        iw_b = iw * ((IDX_HEADS ** -0.5) * (IDX_DIM ** -0.5))
        o_b = dsa_attention(q_b, c_kv, iq_b, ik, iw_b)
        o_b = jnp.einsum('blhc,hcd->blhd', o_b, w_uv[i]).reshape(B, L, DSA_HEADS * DSA_HEAD_DIM)
        y_b = o_b @ w_dsa_proj[i]

        gate_a, gate_b = jnp.split(gates, 2, axis=-1)
        mixed = (jax.nn.sigmoid(gate_a) * y_a + jax.nn.sigmoid(gate_b) * y_b) @ w_out[i]
        x = layer_norm(DEEPNORM_ALPHA * x + mixed, ln1_g[i], ln1_b[i])

        h = causal_dwconv(x @ w_up[i], conv_w[i], conv_b[i])
        h_gate, h_val = jnp.split(h, 2, axis=-1)
        ffn = (jax.nn.silu(h_gate) * h_val) @ w_down[i]
        x = layer_norm(DEEPNORM_ALPHA * x + ffn, ln2_g[i], ln2_b[i])

        ple = jax.nn.sigmoid(x @ w_ple_gate[i]) * (p[i] @ w_ple[i])
        x = layer_norm(DEEPNORM_ALPHA * x + ple, ln3_g[i], ln3_b[i])
    return x
```

```python
import functools

import jax
import jax.numpy as jnp
from jax import lax
from jax.experimental import pallas as pl
from jax.experimental.pallas import tpu as pltpu

F32 = jnp.float32
BF16 = jnp.bfloat16
I32 = jnp.int32

D_MODEL = 1024
GLA_HEADS = 4
GLA_DK = 128
GLA_DV = 256
GLA_GATE_RANK = 16
GLA_GATE_TEMP = 16.0
GLA_CHUNK = 64
DSA_HEADS = 16
DSA_HEAD_DIM = 64
DSA_LATENT = 128
IDX_HEADS = 8
IDX_DIM = 64
TOPK_MAX = 256
D_FF = 2816
CONV_W = 3
LN_EPS = 1e-5
PLE_DIM = 256

LANES = 128
INT_MIN = -2147483648
MASK_LOW31 = 0x7FFFFFFF

PROJ_COLS = 9216
COL_DQ = 0
COL_GATE_A = 2048
COL_GATE_B = 3072
COL_GV = 4096
COL_GR = 5120
COL_IQ = 6144
COL_GQ = 7168
COL_GK = 7680
COL_CKV = 8192
COL_IK = 8320
COL_SMALL = 8448
SMALL_IW = GLA_GATE_RANK

NT_DIMS = (((1,), (1,)), ((), ()))
TN_DIMS = (((0,), (0,)), ((), ()))


def _layer_norm(v, g, b):
    mu = jnp.mean(v, axis=-1, keepdims=True)
    d = v - mu
    var = jnp.mean(d * d, axis=-1, keepdims=True)
    return d * lax.rsqrt(var + LN_EPS) * g + b


def _sigmoid(v):
    return 1.0 / (1.0 + jnp.exp(-v))


def _proj_body(x_ref, w_ref, o_ref):
    o_ref[...] = jnp.dot(x_ref[...].astype(BF16), w_ref[...], preferred_element_type=F32)


def _proj(x2d, w_pad):
    T, D = x2d.shape
    N = w_pad.shape[1]
    tm = min(1024, T)
    tn = 1024
    return pl.pallas_call(
        _proj_body,
        grid=(T // tm, N // tn),
        in_specs=[pl.BlockSpec((tm, D), lambda i, j: (i, 0)),
                  pl.BlockSpec((D, tn), lambda i, j: (0, j))],
        out_specs=pl.BlockSpec((tm, tn), lambda i, j: (i, j)),
        out_shape=jax.ShapeDtypeStruct((T, N), F32),
        compiler_params=pltpu.CompilerParams(
            dimension_semantics=("parallel", "arbitrary"),
            vmem_limit_bytes=40 * 1024 * 1024),
        name="proj",
    )(x2d, w_pad)


def _gla_body(gq_ref, gk_ref, gv_ref, gr_ref, small_ref, gate_ref, wup_ref, bup_ref, gnorm_ref,
              wproj_ref, out_ref, st_ref, oa_ref, *, tt):
    @pl.when(pl.program_id(1) == 0)
    def _():
        st_ref[...] = jnp.zeros(st_ref.shape, F32)

    C = GLA_CHUNK
    row = lax.broadcasted_iota(I32, (C, C), 0)
    col = lax.broadcasted_iota(I32, (C, C), 1)
    tril = row >= col
    tril_f = tril.astype(F32)
    for c in range(tt // C):
        r = slice(c * C, (c + 1) * C)
        z = jnp.dot(small_ref[r, :].astype(BF16), wup_ref[...], preferred_element_type=F32) + bup_ref[...]
        log_a = (jnp.minimum(z, 0.0) - jnp.log(1.0 + jnp.exp(-jnp.abs(z)))) * (1.0 / GLA_GATE_TEMP)
        bc = jnp.dot(tril_f, log_a, preferred_element_type=F32, precision=lax.Precision.HIGHEST)
        bl = bc[C - 1:C, :]
        e_b = jnp.exp(bc)
        e_nb = jnp.exp(-bc)
        e_bl = jnp.exp(bl - bc)
        dec = jnp.exp(bl)
        for h in range(GLA_HEADS):
            ks = slice(h * GLA_DK, (h + 1) * GLA_DK)
            vs = slice(h * GLA_DV, (h + 1) * GLA_DV)
            q_in = (gq_ref[r, ks] * (GLA_DK ** -0.5) * e_b[:, ks]).astype(BF16)
            k = gk_ref[r, ks]
            k_in = (k * e_nb[:, ks]).astype(BF16)
            k_st = (k * e_bl[:, ks]).astype(BF16)
            v = gv_ref[r, vs].astype(BF16)
            att = lax.dot_general(q_in, k_in, NT_DIMS, preferred_element_type=F32)
            att = jnp.where(tril, att, 0.0).astype(BF16)
            st = st_ref[h]
            o = (jnp.dot(att, v, preferred_element_type=F32)
                 + lax.dot_general(q_in, st.astype(BF16), NT_DIMS, preferred_element_type=F32))
            st_ref[h] = st * dec[:, ks] + lax.dot_general(v, k_st, TN_DIMS, preferred_element_type=F32)
            ms = jnp.mean(o * o, axis=-1, keepdims=True)
            on = o * lax.rsqrt(ms + LN_EPS) * gnorm_ref[:, vs]
            g = gr_ref[r, vs]
            oa_ref[r, vs] = (on * (g * _sigmoid(g))).astype(BF16)
    y = jnp.dot(oa_ref[...], wproj_ref[...], preferred_element_type=F32)
    out_ref[...] = _sigmoid(gate_ref[...]) * y


def _gla(proj, B, L, wup_pad, bup, gnorm, wproj):
    T = B * L
    tt = 256
    nt = L // tt

    def blk(width, col):
        cb = col // width
        return pl.BlockSpec((tt, width), lambda b, i: (b * nt + i, cb))

    def full(shape):
        return pl.BlockSpec(shape, lambda b, i: (0,) * len(shape))

    return pl.pallas_call(
        functools.partial(_gla_body, tt=tt),
        grid=(B, nt),
        in_specs=[blk(512, COL_GQ), blk(512, COL_GK), blk(1024, COL_GV), blk(1024, COL_GR),
                  blk(128, COL_SMALL), blk(1024, COL_GATE_A),
                  full(wup_pad.shape), full(bup.shape), full(gnorm.shape), full(wproj.shape)],
        out_specs=pl.BlockSpec((tt, D_MODEL), lambda b, i: (b * nt + i, 0)),
        out_shape=jax.ShapeDtypeStruct((T, D_MODEL), F32),
        scratch_shapes=[pltpu.VMEM((GLA_HEADS, GLA_DV, GLA_DK), F32),
                        pltpu.VMEM((tt, GLA_HEADS * GLA_DV), BF16)],
        compiler_params=pltpu.CompilerParams(
            dimension_semantics=("parallel", "arbitrary"),
            vmem_limit_bytes=40 * 1024 * 1024),
        name="gla",
    )(proj, proj, proj, proj, proj, proj, wup_pad, bup, gnorm, wproj)


def _dsa_body(dq_ref, iq_ref, ckv_ref, ik_ref, small_ref, gate_ref, gck_ref, wuv_ref, wdsa_ref,
              out_ref, kv_ref, ikk_ref, keys_ref, q2_ref, iq2_ref, wb_ref, acc_ref, m_ref,
              *, L, tq, tk, topk):
    qb = pl.program_id(1)
    q0 = pl.multiple_of(qb * tq, tq)
    n_tiles = q0 // tk + 1
    nsub = tk // LANES

    @pl.when(qb == 0)
    def _():
        kv_ref[...] = jnp.zeros(kv_ref.shape, BF16)
        ikk_ref[...] = jnp.zeros(ikk_ref.shape, BF16)

    c = ckv_ref[...]
    cn = c * lax.rsqrt(jnp.mean(c * c, axis=-1, keepdims=True) + LN_EPS) * gck_ref[...]
    kv_ref[pl.ds(q0, tq), 0:DSA_LATENT] = cn.astype(BF16)
    kv_ref[pl.ds(q0, tq), DSA_LATENT:2 * DSA_LATENT] = jnp.ones((tq, DSA_LATENT), BF16)
    ikk_ref[pl.ds(q0, tq), :] = ik_ref[...].astype(BF16)

    for h in range(DSA_HEADS):
        q2_ref[h * tq:(h + 1) * tq, :] = (
            dq_ref[:, h * DSA_LATENT:(h + 1) * DSA_LATENT] * (DSA_LATENT ** -0.5)).astype(BF16)
    for h in range(IDX_HEADS):
        iq2_ref[h * tq:(h + 1) * tq, :] = iq_ref[:, h * LANES:(h + 1) * LANES].astype(BF16)
    sm = small_ref[...]
    iw_scale = (IDX_HEADS ** -0.5) * (IDX_DIM ** -0.5)
    for h in range(IDX_HEADS):
        wb_ref[h] = jnp.broadcast_to(sm[:, SMALL_IW + h:SMALL_IW + h + 1] * iw_scale, (tq, LANES))

    tpos = q0 + lax.broadcasted_iota(I32, (tq, tk), 0)
    lane = lax.broadcasted_iota(I32, (tq, tk), 1)
    lane_slab = lax.broadcasted_iota(I32, (tq, LANES), 1)

    def score_tile(kt, carry):
        k0 = pl.multiple_of(kt * tk, tk)
        s_all = lax.dot_general(iq2_ref[...], ikk_ref[pl.ds(k0, tk), :], NT_DIMS,
                                preferred_element_type=F32)
        sc = jnp.zeros((tq, tk), F32)
        for h in range(IDX_HEADS):
            wbh = jnp.concatenate([wb_ref[h]] * nsub, axis=1)
            sc = sc + wbh * jnp.maximum(s_all[h * tq:(h + 1) * tq, :], 0.0)
        bits = lax.bitcast_convert_type(sc, I32)
        key = jnp.where(bits < 0, bits ^ MASK_LOW31, bits)
        keys_ref[kt] = jnp.where(k0 + lane <= tpos, key, INT_MIN)
        return carry

    lax.fori_loop(0, n_tiles, score_tile, 0)

    def count(pred):
        def body(kt, acc):
            tile = keys_ref[kt]
            for j in range(nsub):
                sidx = kt * tk + j * LANES + lane_slab
                acc = acc + jnp.where(pred(tile[:, j * LANES:(j + 1) * LANES], sidx), 1.0, 0.0)
            return acc
        acc = lax.fori_loop(0, n_tiles, body, jnp.zeros((tq, LANES), F32))
        return jnp.sum(acc, axis=1, keepdims=True)

    def bcast(v):
        return jnp.broadcast_to(v, (tq, LANES))

    def value_pass(p, thr):
        cand = thr + lax.shift_left(jnp.int32(1), 31 - p)
        cb = bcast(cand)
        cnt = count(lambda k, s: k >= cb)
        return jnp.where(cnt >= topk, cand, thr)

    thr = lax.fori_loop(0, 32, value_pass, jnp.full((tq, 1), INT_MIN, I32))
    tb = bcast(thr)

    n_gt = count(lambda k, s: k > tb)
    n_ge = count(lambda k, s: k >= tb)
    need = topk - n_gt
    is_min = thr == INT_MIN
    partial = jnp.logical_and(n_ge - n_gt > need, jnp.logical_not(is_min))
    any_partial = jnp.max(jnp.where(partial, 1.0, 0.0)) > 0.0
    nbits = max(1, (L - 1).bit_length())

    def tie_search():
        def idx_pass(p, d):
            cand = d + lax.shift_left(jnp.int32(1), nbits - 1 - p)
            cb = bcast(cand)
            cnt = count(lambda k, s: jnp.logical_and(k == tb, s < cb))
            return jnp.where(cnt < need, cand, d)
        return lax.fori_loop(0, nbits, idx_pass, jnp.zeros((tq, 1), I32))

    cut = lax.cond(any_partial, tie_search, lambda: jnp.zeros((tq, 1), I32))
    cut = jnp.where(partial, cut, L)
    cut = jnp.where(is_min, -1, cut)
    thr_t = jnp.broadcast_to(thr, (tq, tk))
    cut_t = jnp.broadcast_to(cut, (tq, tk))

    acc_ref[...] = jnp.zeros(acc_ref.shape, F32)
    m_ref[...] = jnp.full(m_ref.shape, -1e30, F32)

    def att_tile(kt, carry):
        k0 = pl.multiple_of(kt * tk, tk)
        kvt = kv_ref[pl.ds(k0, tk), :]
        key = keys_ref[kt]
        sel = jnp.logical_or(key > thr_t,
                             jnp.logical_and(key == thr_t, k0 + lane <= cut_t))
        bias = jnp.where(sel, 0.0, -jnp.inf)
        logits = lax.dot_general(q2_ref[...], kvt[:, 0:DSA_LATENT], NT_DIMS, preferred_element_type=F32)
        logits = (logits.reshape(DSA_HEADS, tq, tk) + bias[None]).reshape(DSA_HEADS * tq, tk)
        m_old = m_ref[...]
        m_new = jnp.maximum(m_old, jnp.max(logits, axis=1, keepdims=True))
        alpha = jnp.exp(m_old - m_new)
        p = jnp.exp(logits - m_new).astype(BF16)
        acc_ref[...] = acc_ref[...] * alpha + jnp.dot(p, kvt, preferred_element_type=F32)
        m_ref[...] = m_new
        return carry

    lax.fori_loop(0, n_tiles, att_tile, 0)

    acc = acc_ref[...]
    o = acc[:, 0:DSA_LATENT] / acc[:, DSA_LATENT:2 * DSA_LATENT]
    y = jnp.zeros((tq, D_MODEL), F32)
    for h in range(DSA_HEADS):
        u = jnp.dot(o[h * tq:(h + 1) * tq, :].astype(BF16), wuv_ref[h], preferred_element_type=F32)
        y = y + jnp.dot(u.astype(BF16), wdsa_ref[h], preferred_element_type=F32)
    out_ref[...] = _sigmoid(gate_ref[...]) * y


def _dsa(proj, B, L, gck, wuv, wdsa):
    T = B * L
    tq = 128
    tk = min(512, L)
    nq = L // tq
    topk = min(TOPK_MAX, L // 4)

    def blk(width, col):
        cb = col // width
        return pl.BlockSpec((tq, width), lambda b, i: (b * nq + i, cb))

    def full(shape):
        return pl.BlockSpec(shape, lambda b, i: (0,) * len(shape))

    return pl.pallas_call(
        functools.partial(_dsa_body, L=L, tq=tq, tk=tk, topk=topk),
        grid=(B, nq),
        in_specs=[blk(2048, COL_DQ), blk(1024, COL_IQ), blk(128, COL_CKV), blk(128, COL_IK),
                  blk(128, COL_SMALL), blk(1024, COL_GATE_B),
                  full(gck.shape), full(wuv.shape), full(wdsa.shape)],
        out_specs=pl.BlockSpec((tq, D_MODEL), lambda b, i: (b * nq + i, 0)),
        out_shape=jax.ShapeDtypeStruct((T, D_MODEL), F32),
        scratch_shapes=[pltpu.VMEM((L, 2 * DSA_LATENT), BF16),
                        pltpu.VMEM((L, LANES), BF16),
                        pltpu.VMEM((L // tk, tq, tk), I32),
                        pltpu.VMEM((DSA_HEADS * tq, DSA_LATENT), BF16),
                        pltpu.VMEM((IDX_HEADS * tq, LANES), BF16),
                        pltpu.VMEM((IDX_HEADS, tq, LANES), F32),
                        pltpu.VMEM((DSA_HEADS * tq, 2 * DSA_LATENT), F32),
                        pltpu.VMEM((DSA_HEADS * tq, 1), F32)],
        compiler_params=pltpu.CompilerParams(
            dimension_semantics=("parallel", "arbitrary"),
            vmem_limit_bytes=56 * 1024 * 1024),
        name="dsa",
    )(proj, proj, proj, proj, proj, proj, gck, wuv, wdsa)


def _merge_body(a_ref, b_ref, x_ref, w_ref, g_ref, be_ref, o_ref, *, alpha):
    s = (a_ref[...] + b_ref[...]).astype(BF16)
    mixed = jnp.dot(s, w_ref[...], preferred_element_type=F32)
    o_ref[...] = _layer_norm(alpha * x_ref[...] + mixed, g_ref[...], be_ref[...])


def _merge(ya, yb, x2d, w_out, g, b, alpha):
    T = x2d.shape[0]
    tm = min(512, T)
    row = pl.BlockSpec((tm, D_MODEL), lambda i: (i, 0))

    def full(shape):
        return pl.BlockSpec(shape, lambda i: (0,) * len(shape))

    return pl.pallas_call(
        functools.partial(_merge_body, alpha=alpha),
        grid=(T // tm,),
        in_specs=[row, row, row, full(w_out.shape), full(g.shape), full(b.shape)],
        out_specs=row,
        out_shape=jax.ShapeDtypeStruct((T, D_MODEL), F32),
        compiler_params=pltpu.CompilerParams(
            dimension_semantics=("parallel",), vmem_limit_bytes=40 * 1024 * 1024),
        name="merge",
    )(ya, yb, x2d, w_out, g, b)


FFN_HALO = 16


def _ffn_body(xm_ref, xh_ref, wg_ref, wv_ref, cwg_ref, cwv_ref, cbg_ref, cbv_ref, wd_ref,
              ln2g_ref, ln2b_ref, p_ref, wple_ref, wpg_ref, ln3g_ref, ln3b_ref,
              o_ref, acc_ref, hg_ref, hv_ref, xb_ref, *, tm, L, nj, alpha):
    i = pl.program_id(0)
    j = pl.program_id(1)

    @pl.when(j == 0)
    def _():
        acc_ref[...] = jnp.zeros(acc_ref.shape, F32)
        at_start = (i * tm) % L == 0
        halo = jnp.where(at_start, 0.0, xh_ref[...])
        xb_ref[0:FFN_HALO, :] = halo.astype(BF16)
        xb_ref[FFN_HALO:FFN_HALO + tm, :] = xm_ref[...].astype(BF16)

    xb = xb_ref[...]

    def conv_branch(w_ref, cw_ref, cb_ref, h_ref):
        h_ref[...] = jnp.dot(xb, w_ref[...], preferred_element_type=F32)
        cw = cw_ref[...]
        return (cw[2:3, :] * h_ref[FFN_HALO:FFN_HALO + tm, :]
                + cw[0:1, :] * h_ref[FFN_HALO - 2:FFN_HALO - 2 + tm, :]
                + cw[1:2, :] * h_ref[FFN_HALO - 1:FFN_HALO - 1 + tm, :]
                + cb_ref[...])

    hg = conv_branch(wg_ref, cwg_ref, cbg_ref, hg_ref)
    hv = conv_branch(wv_ref, cwv_ref, cbv_ref, hv_ref)
    act = ((hg * _sigmoid(hg)) * hv).astype(BF16)
    acc_ref[...] += jnp.dot(act, wd_ref[...], preferred_element_type=F32)

    @pl.when(j == nj - 1)
    def _():
        x2 = _layer_norm(alpha * xm_ref[...] + acc_ref[...], ln2g_ref[...], ln2b_ref[...])
        gate = _sigmoid(jnp.dot(x2.astype(BF16), wpg_ref[...], preferred_element_type=F32))
        ple = gate * jnp.dot(p_ref[...].astype(BF16), wple_ref[...], preferred_element_type=F32)
        o_ref[...] = _layer_norm(alpha * x2 + ple, ln3g_ref[...], ln3b_ref[...])


def _ffn(x1, L, w_up, conv_w, conv_b, w_down, ln2g, ln2b, p2d, w_ple, w_pg, ln3g, ln3b, alpha):
    T = x1.shape[0]
    tm = min(512, L)
    tf = 256
    nj = D_FF // tf
    hb = tm // FFN_HALO

    def full(shape):
        return pl.BlockSpec(shape, lambda i, j: (0,) * len(shape))

    return pl.pallas_call(
        functools.partial(_ffn_body, tm=tm, L=L, nj=nj, alpha=alpha),
        grid=(T // tm, nj),
        in_specs=[pl.BlockSpec((tm, D_MODEL), lambda i, j: (i, 0)),
                  pl.BlockSpec((FFN_HALO, D_MODEL), lambda i, j: (jnp.maximum(i * hb - 1, 0), 0)),
                  pl.BlockSpec((D_MODEL, tf), lambda i, j: (0, j)),
                  pl.BlockSpec((D_MODEL, tf), lambda i, j: (0, nj + j)),
                  pl.BlockSpec((CONV_W, tf), lambda i, j: (0, j)),
                  pl.BlockSpec((CONV_W, tf), lambda i, j: (0, nj + j)),
                  pl.BlockSpec((1, tf), lambda i, j: (0, j)),
                  pl.BlockSpec((1, tf), lambda i, j: (0, nj + j)),
                  pl.BlockSpec((tf, D_MODEL), lambda i, j: (j, 0)),
                  full(ln2g.shape), full(ln2b.shape),
                  pl.BlockSpec((tm, PLE_DIM), lambda i, j: (i, 0)),
                  full(w_ple.shape), full(w_pg.shape), full(ln3g.shape), full(ln3b.shape)],
        out_specs=pl.BlockSpec((tm, D_MODEL), lambda i, j: (i, 0)),
        out_shape=jax.ShapeDtypeStruct((T, D_MODEL), F32),
        scratch_shapes=[pltpu.VMEM((tm, D_MODEL), F32),
                        pltpu.VMEM((tm + FFN_HALO, tf), F32),
                        pltpu.VMEM((tm + FFN_HALO, tf), F32),
                        pltpu.VMEM((tm + FFN_HALO, D_MODEL), BF16)],
        compiler_params=pltpu.CompilerParams(
            dimension_semantics=("parallel", "arbitrary"),
            vmem_limit_bytes=48 * 1024 * 1024),
        name="ffn",
    )(x1, x1, w_up, w_up, conv_w, conv_w, conv_b, conv_b, w_down, ln2g, ln2b, p2d, w_ple, w_pg,
      ln3g, ln3b)


def _permute_w_in(w):
    D = w.shape[0]
    o = 0
    seg = {}
    for name, width in (("gq", 512), ("gk", 512), ("gv", 1024), ("gr", 1024), ("ga", GLA_GATE_RANK),
                        ("dq", 2048), ("ckv", 128), ("iq", 512), ("ik", 64), ("iw", 8),
                        ("gate_a", 1024), ("gate_b", 1024)):
        seg[name] = w[:, o:o + width]
        o += width
    z = lambda n: jnp.zeros((D, n), w.dtype)
    iq_pad = jnp.concatenate(
        [jnp.concatenate([seg["iq"][:, h * IDX_DIM:(h + 1) * IDX_DIM], z(LANES - IDX_DIM)], axis=1)
         for h in range(IDX_HEADS)], axis=1)
    small = jnp.concatenate([seg["ga"], seg["iw"], z(LANES - GLA_GATE_RANK - IDX_HEADS)], axis=1)
    parts = [seg["dq"], seg["gate_a"], seg["gate_b"], seg["gv"], seg["gr"], iq_pad, seg["gq"], seg["gk"],
             seg["ckv"], jnp.concatenate([seg["ik"], z(LANES - IDX_DIM)], axis=1), small]
    used = sum(p.shape[1] for p in parts)
    parts.append(z(PROJ_COLS - used))
    return jnp.concatenate(parts, axis=1).astype(BF16)


def kernel(x, p, w_in, w_gla_gate_up, b_gla_gate, g_gla_norm, w_gla_proj, g_ckv_norm, w_uv, w_dsa_proj,
           w_out, ln1_g, ln1_b, w_up, conv_w, conv_b, w_down, ln2_g, ln2_b, w_ple, w_ple_gate, ln3_g,
           ln3_b):
    B, L, D = x.shape
    depth = w_in.shape[0]
    alpha = (2.0 * depth) ** 0.25
    T = B * L
    x2d = x.reshape(T, D)
    row = lambda v: v.reshape(1, -1)
    for i in range(depth):
        w_pad = _permute_w_in(w_in[i])
        proj = _proj(x2d, w_pad)
        wup_pad = jnp.concatenate(
            [w_gla_gate_up[i], jnp.zeros((LANES - GLA_GATE_RANK, GLA_HEADS * GLA_DK), F32)], axis=0).astype(BF16)
        ya = _gla(proj, B, L, wup_pad, row(b_gla_gate[i]), row(g_gla_norm[i]), w_gla_proj[i].astype(BF16))
        yb = _dsa(proj, B, L, row(g_ckv_norm[i]), w_uv[i].astype(BF16),
                  w_dsa_proj[i].reshape(DSA_HEADS, DSA_HEAD_DIM, D_MODEL).astype(BF16))
        x1 = _merge(ya, yb, x2d, w_out[i].astype(BF16), row(ln1_g[i]), row(ln1_b[i]), alpha)
        x2d = _ffn(x1, L, w_up[i].astype(BF16), conv_w[i], row(conv_b[i]), w_down[i].astype(BF16),
                   row(ln2_g[i]), row(ln2_b[i]), p[i].reshape(T, PLE_DIM), w_ple[i].astype(BF16),
                   w_ple_gate[i].astype(BF16), row(ln3_g[i]), row(ln3_b[i]), alpha)
    return x2d.reshape(B, L, D)
```

```python
import functools

import jax
import jax.numpy as jnp
from jax import lax
from jax.experimental import pallas as pl
from jax.experimental.pallas import tpu as pltpu

F32 = jnp.float32
BF16 = jnp.bfloat16
I32 = jnp.int32

D_MODEL = 1024
GLA_HEADS = 4
GLA_DK = 128
GLA_DV = 256
GLA_GATE_RANK = 16
GLA_GATE_TEMP = 16.0
GLA_CHUNK = 64
DSA_HEADS = 16
DSA_HEAD_DIM = 64
DSA_LATENT = 128
IDX_HEADS = 8
IDX_DIM = 64
TOPK_MAX = 256
D_FF = 2816
CONV_W = 3
LN_EPS = 1e-5
PLE_DIM = 256

LANES = 128
SUBLANES = 8
INT_MIN = -2147483648
MASK_LOW31 = 0x7FFFFFFF
MASKED_LOGIT = -1e30
MAX_SAFE_SHIFT = 40.0
SHIFT_SLACK = 1.02

PROJ_COLS = 8192
COL_DQ = 0
COL_GATE_A = 2048
COL_GATE_B = 3072
COL_GV = 4096
COL_GR = 5120
COL_GQ = 6144
COL_GK = 6656
COL_IQ = 7168
COL_CKV = 7680
COL_IK = 7808
COL_SMALL = 7936
SMALL_IW = GLA_GATE_RANK

NT_DIMS = (((1,), (1,)), ((), ()))
TN_DIMS = (((0,), (0,)), ((), ()))


def _layer_norm(v, g, b):
    mu = jnp.mean(v, axis=-1, keepdims=True)
    d = v - mu
    var = jnp.mean(d * d, axis=-1, keepdims=True)
    return d * lax.rsqrt(var + LN_EPS) * g + b


def _sigmoid(v):
    return 1.0 / (1.0 + jnp.exp(-v))


def _proj_body(x_ref, w_ref, o_ref):
    o_ref[...] = jnp.dot(x_ref[...].astype(BF16), w_ref[...], preferred_element_type=F32)


def _proj(x2d, w_pad):
    T, D = x2d.shape
    N = w_pad.shape[1]
    tm = min(1024, T)
    tn = 1024
    return pl.pallas_call(
        _proj_body,
        grid=(T // tm, N // tn),
        in_specs=[pl.BlockSpec((tm, D), lambda i, j: (i, 0)),
                  pl.BlockSpec((D, tn), lambda i, j: (0, j))],
        out_specs=pl.BlockSpec((tm, tn), lambda i, j: (i, j)),
        out_shape=jax.ShapeDtypeStruct((T, N), F32),
        compiler_params=pltpu.CompilerParams(
            dimension_semantics=("parallel", "arbitrary"),
            vmem_limit_bytes=40 * 1024 * 1024),
        name="proj",
    )(x2d, w_pad)


def _gla_body(gq_ref, gk_ref, gv_ref, gr_ref, small_ref, gate_ref, wup_ref, bup_ref, gnorm_ref,
              wproj_ref, out_ref, st_ref, oa_ref, *, tt):
    @pl.when(pl.program_id(1) == 0)
    def _():
        st_ref[...] = jnp.zeros(st_ref.shape, F32)

    C = GLA_CHUNK
    row = lax.broadcasted_iota(I32, (C, C), 0)
    col = lax.broadcasted_iota(I32, (C, C), 1)
    tril = row >= col
    tril_f = tril.astype(F32)
    for c in range(tt // C):
        r = slice(c * C, (c + 1) * C)
        z = jnp.dot(small_ref[r, :].astype(BF16), wup_ref[...], preferred_element_type=F32) + bup_ref[...]
        log_a = (jnp.minimum(z, 0.0) - jnp.log(1.0 + jnp.exp(-jnp.abs(z)))) * (1.0 / GLA_GATE_TEMP)
        bc = jnp.dot(tril_f, log_a, preferred_element_type=F32, precision=lax.Precision.HIGHEST)
        bl = bc[C - 1:C, :]
        e_b = jnp.exp(bc)
        e_nb = jnp.exp(-bc)
        e_bl = jnp.exp(bl - bc)
        dec = jnp.exp(bl)
        for h in range(GLA_HEADS):
            ks = slice(h * GLA_DK, (h + 1) * GLA_DK)
            vs = slice(h * GLA_DV, (h + 1) * GLA_DV)
            q_in = (gq_ref[r, ks] * (GLA_DK ** -0.5) * e_b[:, ks]).astype(BF16)
            k = gk_ref[r, ks]
            k_in = (k * e_nb[:, ks]).astype(BF16)
            k_st = (k * e_bl[:, ks]).astype(BF16)
            v = gv_ref[r, vs].astype(BF16)
            att = lax.dot_general(q_in, k_in, NT_DIMS, preferred_element_type=F32)
            att = jnp.where(tril, att, 0.0).astype(BF16)
            st = st_ref[h]
            o = (jnp.dot(att, v, preferred_element_type=F32)
                 + lax.dot_general(q_in, st.astype(BF16), NT_DIMS, preferred_element_type=F32))
            st_ref[h] = st * dec[:, ks] + lax.dot_general(v, k_st, TN_DIMS, preferred_element_type=F32)
            ms = jnp.mean(o * o, axis=-1, keepdims=True)
            on = o * lax.rsqrt(ms + LN_EPS) * gnorm_ref[:, vs]
            g = gr_ref[r, vs]
            oa_ref[r, vs] = (on * (g * _sigmoid(g))).astype(BF16)
    y = jnp.dot(oa_ref[...], wproj_ref[...], preferred_element_type=F32)
    out_ref[...] = _sigmoid(gate_ref[...]) * y


def _gla(proj, B, L, wup_pad, bup, gnorm, wproj):
    T = B * L
    tt = 256
    nt = L // tt

    def blk(width, col):
        cb = col // width
        return pl.BlockSpec((tt, width), lambda b, i: (b * nt + i, cb))

    def full(shape):
        return pl.BlockSpec(shape, lambda b, i: (0,) * len(shape))

    return pl.pallas_call(
        functools.partial(_gla_body, tt=tt),
        grid=(B, nt),
        in_specs=[blk(512, COL_GQ), blk(512, COL_GK), blk(1024, COL_GV), blk(1024, COL_GR),
                  blk(128, COL_SMALL), blk(1024, COL_GATE_A),
                  full(wup_pad.shape), full(bup.shape), full(gnorm.shape), full(wproj.shape)],
        out_specs=pl.BlockSpec((tt, D_MODEL), lambda b, i: (b * nt + i, 0)),
        out_shape=jax.ShapeDtypeStruct((T, D_MODEL), F32),
        scratch_shapes=[pltpu.VMEM((GLA_HEADS, GLA_DV, GLA_DK), F32),
                        pltpu.VMEM((tt, GLA_HEADS * GLA_DV), BF16)],
        compiler_params=pltpu.CompilerParams(
            dimension_semantics=("parallel", "arbitrary"),
            vmem_limit_bytes=40 * 1024 * 1024),
        name="gla",
    )(proj, proj, proj, proj, proj, proj, wup_pad, bup, gnorm, wproj)


def _dsa_body(dq_ref, iq_ref, ckv_ref, ik_ref, small_ref, gck_ref,
              out_ref, kv_ref, ikk_ref, keys_ref, q2_ref, iq2t_ref, acc_ref, m_ref,
              *, L, tq, tk, topk):
    qb = pl.program_id(1)
    q0 = pl.multiple_of(qb * tq, tq)
    n_tiles = q0 // tk + 1
    dl = DSA_LATENT

    @pl.when(qb == 0)
    def _():
        kv_ref[...] = jnp.zeros(kv_ref.shape, BF16)
        ikk_ref[...] = jnp.zeros(ikk_ref.shape, BF16)
        onehot = (lax.broadcasted_iota(I32, (tq, tq), 0) == lax.broadcasted_iota(I32, (tq, tq), 1))
        for h in range(DSA_HEADS):
            q2_ref[h * tq:(h + 1) * tq, dl:2 * dl] = jnp.where(onehot, 1.0, 0.0).astype(BF16)

    c = ckv_ref[...]
    cn = c * lax.rsqrt(jnp.mean(c * c, axis=-1, keepdims=True) + LN_EPS) * gck_ref[...]
    kv_ref[pl.ds(q0, tq), 0:dl] = cn.astype(BF16)
    kv_ref[pl.ds(q0, tq), dl:2 * dl] = jnp.ones((tq, dl), BF16)
    ikk_ref[pl.ds(q0, tq), :] = ik_ref[...].astype(BF16)

    qn2 = jnp.zeros((tq, 1), F32)
    for h in range(DSA_HEADS):
        qh = dq_ref[:, h * dl:(h + 1) * dl] * (dl ** -0.5)
        q2_ref[h * tq:(h + 1) * tq, 0:dl] = qh.astype(BF16)
        qn2 = jnp.maximum(qn2, jnp.sum(qh * qh, axis=1, keepdims=True))
    for h in range(IDX_HEADS):
        iqh = jnp.concatenate(
            [iq_ref[:, h * IDX_DIM:(h + 1) * IDX_DIM], jnp.zeros((tq, LANES - IDX_DIM), F32)], axis=1)
        iq2t_ref[:, h * tq:(h + 1) * tq] = iqh.T.astype(BF16)
    sm_t = small_ref[...].T
    iw_scale = (IDX_HEADS ** -0.5) * (IDX_DIM ** -0.5)
    w_rows = [sm_t[SMALL_IW + h:SMALL_IW + h + 1, :] * iw_scale for h in range(IDX_HEADS)]
    qn2_row = jnp.broadcast_to(qn2, (tq, LANES)).T[0:1, :]

    kidx0 = lax.broadcasted_iota(I32, (tk, tq), 0)
    qpos = q0 + lax.broadcasted_iota(I32, (tk, tq), 1)

    def score_keys(kt):
        k0 = pl.multiple_of(kt * tk, tk)
        s_all = jnp.dot(ikk_ref[pl.ds(k0, tk), :], iq2t_ref[...], preferred_element_type=F32)
        sc = jnp.zeros((tk, tq), F32)
        for h in range(IDX_HEADS):
            sc = sc + w_rows[h] * jnp.maximum(s_all[:, h * tq:(h + 1) * tq], 0.0)
        bits = lax.bitcast_convert_type(sc, I32)
        key = jnp.where(bits < 0, bits ^ MASK_LOW31, bits)
        keys_ref[kt] = jnp.where(k0 + kidx0 <= qpos, key, INT_MIN)

    def score_pair(i, carry):
        score_keys(2 * i)
        score_keys(2 * i + 1)
        return carry

    lax.fori_loop(0, (n_tiles + 1) // 2, score_pair, 0)

    n_acc = 4

    def count(pred):
        def body(kt, acc):
            hit = jnp.where(pred(keys_ref[kt], kt * tk + kidx0), 1.0, 0.0)
            part = jnp.sum(hit.reshape(tk // (SUBLANES * n_acc), n_acc * SUBLANES, tq), axis=0)
            return acc + part
        acc = lax.fori_loop(0, n_tiles, body, jnp.zeros((n_acc * SUBLANES, tq), F32))
        return jnp.sum(acc, axis=0, keepdims=True)

    def tile_b(v):
        return jnp.broadcast_to(v, (tk, tq))

    def value_pass(p, thr):
        cand = thr + lax.shift_left(jnp.int32(1), 31 - p)
        cb = tile_b(cand)
        cnt = count(lambda k, s: k >= cb)
        return jnp.where(cnt >= topk, cand, thr)

    thr = lax.fori_loop(0, 32, value_pass, jnp.full((1, tq), INT_MIN, I32))
    thr_t = tile_b(thr)

    n_gt = count(lambda k, s: k > thr_t)
    n_ge = count(lambda k, s: k >= thr_t)
    need = topk - n_gt
    is_min = thr == INT_MIN
    partial = jnp.logical_and(n_ge - n_gt > need, jnp.logical_not(is_min))
    any_partial = jnp.max(jnp.where(partial, 1.0, 0.0)) > 0.0
    nbits = max(1, (L - 1).bit_length())

    def tie_search():
        def idx_pass(p, d):
            cand = d + lax.shift_left(jnp.int32(1), nbits - 1 - p)
            cb = tile_b(cand)
            cnt = count(lambda k, s: jnp.logical_and(k == thr_t, s < cb))
            return jnp.where(cnt < need, cand, d)
        return lax.fori_loop(0, nbits, idx_pass, jnp.zeros((1, tq), I32))

    cut = lax.cond(any_partial, tie_search, lambda: jnp.zeros((1, tq), I32))
    cut = jnp.where(partial, cut, L)
    cut = jnp.where(is_min, -1, cut)
    cut_t = tile_b(cut)

    acc_ref[...] = jnp.zeros(acc_ref.shape, F32)

    def selected(kt):
        key = keys_ref[kt]
        return jnp.logical_or(key > thr_t,
                              jnp.logical_and(key == thr_t, kt * tk + kidx0 <= cut_t))

    k_bound = (dl ** 0.5) * SHIFT_SLACK * jnp.max(jnp.abs(gck_ref[...]))
    shift = jnp.sqrt(qn2_row) * k_bound

    def att_bounded():
        neg_shift = tile_b(-shift)

        def tile(kt, carry):
            k0 = pl.multiple_of(kt * tk, tk)
            bias_t = jnp.where(selected(kt), neg_shift, MASKED_LOGIT).astype(BF16)
            rhs = jnp.concatenate([kv_ref[pl.ds(k0, tk), 0:dl], bias_t], axis=1)
            logits = lax.dot_general(q2_ref[...], rhs, NT_DIMS, preferred_element_type=F32)
            p = jnp.exp(logits).astype(BF16)
            acc_ref[...] += jnp.dot(p, kv_ref[pl.ds(k0, tk), :], preferred_element_type=F32)
            return carry

        lax.fori_loop(0, n_tiles, tile, 0)

    def att_online():
        m_ref[...] = jnp.full(m_ref.shape, MASKED_LOGIT, F32)

        def tile(kt, carry):
            k0 = pl.multiple_of(kt * tk, tk)
            kvt = kv_ref[pl.ds(k0, tk), :]
            bias = jnp.where(selected(kt), 0.0, -jnp.inf).T
            logits = lax.dot_general(q2_ref[:, 0:dl], kvt[:, 0:dl], NT_DIMS, preferred_element_type=F32)
            logits = (logits.reshape(DSA_HEADS, tq, tk) + bias[None]).reshape(DSA_HEADS * tq, tk)
            m_old = m_ref[...]
            m_new = jnp.maximum(m_old, jnp.max(logits, axis=1, keepdims=True))
            alpha = jnp.exp(m_old - m_new)
            p = jnp.exp(logits - m_new).astype(BF16)
            acc_ref[...] = acc_ref[...] * alpha + jnp.dot(p, kvt, preferred_element_type=F32)
            m_ref[...] = m_new
            return carry

        lax.fori_loop(0, n_tiles, tile, 0)

    lax.cond(jnp.max(shift) <= MAX_SAFE_SHIFT, att_bounded, att_online)

    acc = acc_ref[...]
    o = acc[:, 0:dl] * (1.0 / acc[:, dl:2 * dl])
    for h in range(DSA_HEADS):
        out_ref[:, h * dl:(h + 1) * dl] = o[h * tq:(h + 1) * tq, :].astype(BF16)


def _dsa(proj, B, L, gck):
    T = B * L
    tq = 128
    tk = min(512, L)
    nq = L // tq
    topk = min(TOPK_MAX, L // 4)
    assert (L // tk) % 2 == 0 and tk >= topk and tq == LANES, "score tiles are visited in pairs"

    def blk(width, col):
        cb = col // width
        return pl.BlockSpec((tq, width), lambda b, i: (b * nq + i, cb))

    def full(shape):
        return pl.BlockSpec(shape, lambda b, i: (0,) * len(shape))

    return pl.pallas_call(
        functools.partial(_dsa_body, L=L, tq=tq, tk=tk, topk=topk),
        grid=(B, nq),
        in_specs=[blk(2048, COL_DQ), blk(512, COL_IQ), blk(128, COL_CKV), blk(128, COL_IK),
                  blk(128, COL_SMALL), full(gck.shape)],
        out_specs=pl.BlockSpec((tq, DSA_HEADS * DSA_LATENT), lambda b, i: (b * nq + i, 0)),
        out_shape=jax.ShapeDtypeStruct((T, DSA_HEADS * DSA_LATENT), BF16),
        scratch_shapes=[pltpu.VMEM((L, 2 * DSA_LATENT), BF16),
                        pltpu.VMEM((L, LANES), BF16),
                        pltpu.VMEM((L // tk, tk, tq), I32),
                        pltpu.VMEM((DSA_HEADS * tq, 2 * DSA_LATENT), BF16),
                        pltpu.VMEM((LANES, IDX_HEADS * tq), BF16),
                        pltpu.VMEM((DSA_HEADS * tq, 2 * DSA_LATENT), F32),
                        pltpu.VMEM((DSA_HEADS * tq, 1), F32)],
        compiler_params=pltpu.CompilerParams(
            dimension_semantics=("parallel", "arbitrary"),
            vmem_limit_bytes=56 * 1024 * 1024),
        name="dsa",
    )(proj, proj, proj, proj, proj, gck)


def _merge_body(a_ref, ob_ref, gate_ref, x_ref, wuv_ref, wdsa_ref, w_ref, g_ref, be_ref, o_ref, *, alpha):
    ob = ob_ref[...]
    pw = 2 * DSA_LATENT
    u = jnp.concatenate(
        [jnp.dot(ob[:, j * pw:(j + 1) * pw], wuv_ref[j], preferred_element_type=F32).astype(BF16)
         for j in range(DSA_HEADS // 2)], axis=1)
    yb = jnp.dot(u, wdsa_ref[...], preferred_element_type=F32)
    s = (a_ref[...] + _sigmoid(gate_ref[...]) * yb).astype(BF16)
    mixed = jnp.dot(s, w_ref[...], preferred_element_type=F32)
    o_ref[...] = _layer_norm(alpha * x_ref[...] + mixed, g_ref[...], be_ref[...])


def _merge(ya, ob, proj, x2d, wuv_pairs, wdsa, w_out, g, b, alpha):
    T = x2d.shape[0]
    tm = min(512, T)
    row = pl.BlockSpec((tm, D_MODEL), lambda i: (i, 0))
    gate_cb = COL_GATE_B // D_MODEL

    def full(shape):
        return pl.BlockSpec(shape, lambda i: (0,) * len(shape))

    return pl.pallas_call(
        functools.partial(_merge_body, alpha=alpha),
        grid=(T // tm,),
        in_specs=[row, pl.BlockSpec((tm, DSA_HEADS * DSA_LATENT), lambda i: (i, 0)),
                  pl.BlockSpec((tm, D_MODEL), lambda i: (i, gate_cb)), row,
                  full(wuv_pairs.shape), full(wdsa.shape), full(w_out.shape), full(g.shape), full(b.shape)],
        out_specs=row,
        out_shape=jax.ShapeDtypeStruct((T, D_MODEL), F32),
        compiler_params=pltpu.CompilerParams(
            dimension_semantics=("parallel",), vmem_limit_bytes=40 * 1024 * 1024),
        name="merge",
    )(ya, ob, proj, x2d, wuv_pairs, wdsa, w_out, g, b)


FFN_HALO = 16
FFN_CHUNK = 256


def _ffn_body(xm_ref, xh_ref, wup_ref, cw_ref, cb_ref, wd_ref, ln2g_ref, ln2b_ref, p_ref, wple_ref,
              wpg_ref, ln3g_ref, ln3b_ref, o_ref, act_ref, h_ref, *, tm, L, alpha):
    i = pl.program_id(0)
    at_start = (i * tm) % L == 0
    halo = jnp.where(at_start, 0.0, xh_ref[...])
    xb = jnp.concatenate([halo, xm_ref[...]], axis=0).astype(BF16)

    def conv_branch(col, slot):
        cs = slice(col, col + FFN_CHUNK)
        h_ref[slot] = jnp.dot(xb, wup_ref[:, cs], preferred_element_type=F32)
        hs = h_ref.at[slot]
        return (cw_ref[2:3, cs] * hs[FFN_HALO:FFN_HALO + tm, :]
                + cw_ref[0:1, cs] * hs[FFN_HALO - 2:FFN_HALO - 2 + tm, :]
                + cw_ref[1:2, cs] * hs[FFN_HALO - 1:FFN_HALO - 1 + tm, :]
                + cb_ref[:, cs])

    for c in range(D_FF // FFN_CHUNK):
        hg = conv_branch(c * FFN_CHUNK, (2 * c) % 4)
        hv = conv_branch(D_FF + c * FFN_CHUNK, (2 * c + 1) % 4)
        act_ref[:, c * FFN_CHUNK:(c + 1) * FFN_CHUNK] = ((hg * _sigmoid(hg)) * hv).astype(BF16)

    ffn = jnp.dot(act_ref[...], wd_ref[...], preferred_element_type=F32)
    x2 = _layer_norm(alpha * xm_ref[...] + ffn, ln2g_ref[...], ln2b_ref[...])
    gate = _sigmoid(jnp.dot(x2.astype(BF16), wpg_ref[...], preferred_element_type=F32))
    ple = gate * jnp.dot(p_ref[...].astype(BF16), wple_ref[...], preferred_element_type=F32)
    o_ref[...] = _layer_norm(alpha * x2 + ple, ln3g_ref[...], ln3b_ref[...])


def _ffn(x1, L, w_up, conv_w, conv_b, w_down, ln2g, ln2b, p2d, w_ple, w_pg, ln3g, ln3b, alpha):
    T = x1.shape[0]
    tm = min(512, L)
    hb = tm // FFN_HALO

    def full(shape):
        return pl.BlockSpec(shape, lambda i: (0,) * len(shape), pipeline_mode=pl.Buffered(1))

    return pl.pallas_call(
        functools.partial(_ffn_body, tm=tm, L=L, alpha=alpha),
        grid=(T // tm,),
        in_specs=[pl.BlockSpec((tm, D_MODEL), lambda i: (i, 0)),
                  pl.BlockSpec((FFN_HALO, D_MODEL), lambda i: (jnp.maximum(i * hb - 1, 0), 0)),
                  full(w_up.shape), full(conv_w.shape), full(conv_b.shape), full(w_down.shape),
                  full(ln2g.shape), full(ln2b.shape),
                  pl.BlockSpec((tm, PLE_DIM), lambda i: (i, 0)),
                  full(w_ple.shape), full(w_pg.shape), full(ln3g.shape), full(ln3b.shape)],
        out_specs=pl.BlockSpec((tm, D_MODEL), lambda i: (i, 0)),
        out_shape=jax.ShapeDtypeStruct((T, D_MODEL), F32),
        scratch_shapes=[pltpu.VMEM((tm, D_FF), BF16),
                        pltpu.VMEM((4, tm + FFN_HALO, FFN_CHUNK), F32)],
        compiler_params=pltpu.CompilerParams(
            dimension_semantics=("parallel",),
            vmem_limit_bytes=52 * 1024 * 1024),
        name="ffn",
    )(x1, x1, w_up, conv_w, conv_b, w_down, ln2g, ln2b, p2d, w_ple, w_pg, ln3g, ln3b)


def _permute_w_in(w):
    D = w.shape[0]
    o = 0
    seg = {}
    for name, width in (("gq", 512), ("gk", 512), ("gv", 1024), ("gr", 1024), ("ga", GLA_GATE_RANK),
                        ("dq", 2048), ("ckv", 128), ("iq", 512), ("ik", 64), ("iw", 8),
                        ("gate_a", 1024), ("gate_b", 1024)):
        seg[name] = w[:, o:o + width]
        o += width
    z = lambda n: jnp.zeros((D, n), w.dtype)
    small = jnp.concatenate([seg["ga"], seg["iw"], z(LANES - GLA_GATE_RANK - IDX_HEADS)], axis=1)
    parts = [seg["dq"], seg["gate_a"], seg["gate_b"], seg["gv"], seg["gr"], seg["gq"], seg["gk"], seg["iq"],
             seg["ckv"], jnp.concatenate([seg["ik"], z(LANES - IDX_DIM)], axis=1), small]
    used = sum(p.shape[1] for p in parts)
    parts.append(z(PROJ_COLS - used))
    return jnp.concatenate(parts, axis=1).astype(BF16)


def _pair_block_diag(w_uv):
    H, dc, dh = w_uv.shape
    z = jnp.zeros((dc, dh), w_uv.dtype)
    blocks = [jnp.concatenate([jnp.concatenate([w_uv[2 * j], z], axis=1),
                               jnp.concatenate([z, w_uv[2 * j + 1]], axis=1)], axis=0)
              for j in range(H // 2)]
    return jnp.stack(blocks).astype(BF16)


def kernel(x, p, w_in, w_gla_gate_up, b_gla_gate, g_gla_norm, w_gla_proj, g_ckv_norm, w_uv, w_dsa_proj,
           w_out, ln1_g, ln1_b, w_up, conv_w, conv_b, w_down, ln2_g, ln2_b, w_ple, w_ple_gate, ln3_g,
           ln3_b):
    B, L, D = x.shape
    depth = w_in.shape[0]
    alpha = (2.0 * depth) ** 0.25
    T = B * L
    x2d = x.reshape(T, D)
    row = lambda v: v.reshape(1, -1)
    for i in range(depth):
        w_pad = _permute_w_in(w_in[i])
        proj = _proj(x2d, w_pad)
        wup_pad = jnp.concatenate(
            [w_gla_gate_up[i], jnp.zeros((LANES - GLA_GATE_RANK, GLA_HEADS * GLA_DK), F32)], axis=0).astype(BF16)
        ya = _gla(proj, B, L, wup_pad, row(b_gla_gate[i]), row(g_gla_norm[i]), w_gla_proj[i].astype(BF16))
        ob = _dsa(proj, B, L, row(g_ckv_norm[i]))
        x1 = _merge(ya, ob, proj, x2d, _pair_block_diag(w_uv[i]), w_dsa_proj[i].astype(BF16),
                    w_out[i].astype(BF16), row(ln1_g[i]), row(ln1_b[i]), alpha)
        x2d = _ffn(x1, L, w_up[i].astype(BF16), conv_w[i], row(conv_b[i]), w_down[i].astype(BF16),
                   row(ln2_g[i]), row(ln2_b[i]), p[i].reshape(T, PLE_DIM), w_ple[i].astype(BF16),
                   w_ple_gate[i].astype(BF16), row(ln3_g[i]), row(ln3_b[i]), alpha)
    return x2d.reshape(B, L, D)
```

```python
import functools

import jax
import jax.numpy as jnp
from jax import lax
from jax.experimental import pallas as pl
from jax.experimental.pallas import tpu as pltpu

F32 = jnp.float32
BF16 = jnp.bfloat16
I32 = jnp.int32

D_MODEL = 1024
GLA_HEADS = 4
GLA_DK = 128
GLA_DV = 256
GLA_GATE_RANK = 16
GLA_GATE_TEMP = 16.0
GLA_CHUNK = 64
DSA_HEADS = 16
DSA_HEAD_DIM = 64
DSA_LATENT = 128
IDX_HEADS = 8
IDX_DIM = 64
TOPK_MAX = 256
D_FF = 2816
CONV_W = 3
LN_EPS = 1e-5
PLE_DIM = 256

LANES = 128
SUBLANES = 8
INT_MIN = -2147483648
MASK_LOW31 = 0x7FFFFFFF
MASKED_LOGIT = -1e30
MAX_SAFE_SHIFT = 40.0
SHIFT_SLACK = 1.02
LOG2_E = 1.4426950408889634

PROJ_COLS = 8192
COL_DQ = 0
COL_GATE_A = 2048
COL_GATE_B = 3072
COL_GV = 4096
COL_GR = 5120
COL_GQ = 6144
COL_GK = 6656
COL_IQ = 7168
COL_CKV = 7680
COL_IK = 7808
COL_SMALL = 7936
SMALL_IW = GLA_GATE_RANK

NT_DIMS = (((1,), (1,)), ((), ()))
TN_DIMS = (((0,), (0,)), ((), ()))


def _layer_norm(v, g, b):
    mu = jnp.mean(v, axis=-1, keepdims=True)
    d = v - mu
    var = jnp.mean(d * d, axis=-1, keepdims=True)
    return d * lax.rsqrt(var + LN_EPS) * g + b


def _sigmoid(v):
    return 1.0 / (1.0 + jnp.exp(-v))


def _proj_body(x_ref, w_ref, o_ref):
    o_ref[...] = jnp.dot(x_ref[...].astype(BF16), w_ref[...], preferred_element_type=F32)


def _proj(x2d, w_pad):
    T, D = x2d.shape
    N = w_pad.shape[1]
    tm = min(1024, T)
    tn = 1024
    return pl.pallas_call(
        _proj_body,
        grid=(T // tm, N // tn),
        in_specs=[pl.BlockSpec((tm, D), lambda i, j: (i, 0)),
                  pl.BlockSpec((D, tn), lambda i, j: (0, j))],
        out_specs=pl.BlockSpec((tm, tn), lambda i, j: (i, j)),
        out_shape=jax.ShapeDtypeStruct((T, N), F32),
        compiler_params=pltpu.CompilerParams(
            dimension_semantics=("parallel", "arbitrary"),
            vmem_limit_bytes=40 * 1024 * 1024),
        name="proj",
    )(x2d, w_pad)


def _gla_body(gq_ref, gk_ref, gv_ref, gr_ref, small_ref, gate_ref, wup_ref, bup_ref, gnorm_ref,
              wproj_ref, out_ref, st_ref, oa_ref, *, tt):
    @pl.when(pl.program_id(1) == 0)
    def _():
        st_ref[...] = jnp.zeros(st_ref.shape, F32)

    C = GLA_CHUNK
    row = lax.broadcasted_iota(I32, (C, C), 0)
    col = lax.broadcasted_iota(I32, (C, C), 1)
    tril = row >= col
    tril_f = tril.astype(F32)
    for c in range(tt // C):
        r = slice(c * C, (c + 1) * C)
        z = jnp.dot(small_ref[r, :].astype(BF16), wup_ref[...], preferred_element_type=F32) + bup_ref[...]
        log_a = (jnp.minimum(z, 0.0) - jnp.log(1.0 + jnp.exp(-jnp.abs(z)))) * (1.0 / GLA_GATE_TEMP)
        bc = jnp.dot(tril_f, log_a, preferred_element_type=F32, precision=lax.Precision.HIGHEST)
        bl = bc[C - 1:C, :]
        e_b = jnp.exp(bc)
        e_nb = jnp.exp(-bc)
        e_bl = jnp.exp(bl - bc)
        dec = jnp.exp(bl)
        for h in range(GLA_HEADS):
            ks = slice(h * GLA_DK, (h + 1) * GLA_DK)
            vs = slice(h * GLA_DV, (h + 1) * GLA_DV)
            q_in = (gq_ref[r, ks] * (GLA_DK ** -0.5) * e_b[:, ks]).astype(BF16)
            k = gk_ref[r, ks]
            k_in = (k * e_nb[:, ks]).astype(BF16)
            k_st = (k * e_bl[:, ks]).astype(BF16)
            v = gv_ref[r, vs].astype(BF16)
            att = lax.dot_general(q_in, k_in, NT_DIMS, preferred_element_type=F32)
            att = jnp.where(tril, att, 0.0).astype(BF16)
            st = st_ref[h]
            o = (jnp.dot(att, v, preferred_element_type=F32)
                 + lax.dot_general(q_in, st.astype(BF16), NT_DIMS, preferred_element_type=F32))
            st_ref[h] = st * dec[:, ks] + lax.dot_general(v, k_st, TN_DIMS, preferred_element_type=F32)
            ms = jnp.mean(o * o, axis=-1, keepdims=True)
            on = o * lax.rsqrt(ms + LN_EPS) * gnorm_ref[:, vs]
            g = gr_ref[r, vs]
            oa_ref[r, vs] = (on * (g * _sigmoid(g))).astype(BF16)
    y = jnp.dot(oa_ref[...], wproj_ref[...], preferred_element_type=F32)
    out_ref[...] = _sigmoid(gate_ref[...]) * y


def _gla(proj, B, L, wup_pad, bup, gnorm, wproj):
    T = B * L
    tt = 256
    nt = L // tt

    def blk(width, col):
        cb = col // width
        return pl.BlockSpec((tt, width), lambda b, i: (b * nt + i, cb))

    def full(shape):
        return pl.BlockSpec(shape, lambda b, i: (0,) * len(shape))

    return pl.pallas_call(
        functools.partial(_gla_body, tt=tt),
        grid=(B, nt),
        in_specs=[blk(512, COL_GQ), blk(512, COL_GK), blk(1024, COL_GV), blk(1024, COL_GR),
                  blk(128, COL_SMALL), blk(1024, COL_GATE_A),
                  full(wup_pad.shape), full(bup.shape), full(gnorm.shape), full(wproj.shape)],
        out_specs=pl.BlockSpec((tt, D_MODEL), lambda b, i: (b * nt + i, 0)),
        out_shape=jax.ShapeDtypeStruct((T, D_MODEL), F32),
        scratch_shapes=[pltpu.VMEM((GLA_HEADS, GLA_DV, GLA_DK), F32),
                        pltpu.VMEM((tt, GLA_HEADS * GLA_DV), BF16)],
        compiler_params=pltpu.CompilerParams(
            dimension_semantics=("parallel", "arbitrary"),
            vmem_limit_bytes=40 * 1024 * 1024),
        name="gla",
    )(proj, proj, proj, proj, proj, proj, wup_pad, bup, gnorm, wproj)


SEARCH_BITS = 32
VISIT_TILES = 1
VISITS_PER_TILE = 32
ATT_CHUNKS = 8


def _dsa_body(dq_ref, iq_ref, ckv_ref, ik_ref, small_ref, gck_ref,
              out_ref, kv_ref, ikk_ref, keys_ref, q2_ref, iq2t_ref, sel_ref, acc_ref, m_ref,
              *, L, tq, tk, topk):
    i = pl.program_id(1)
    nq = L // tq
    nt = L // tk
    dl = DSA_LATENT
    has_b = i < nq
    slot_b = i % 2
    slot_a = 1 - slot_b
    qb = jnp.minimum(i, nq - 1)
    q0 = pl.multiple_of(qb * tq, tq)
    n_b = q0 // tk + 1
    n_a = jnp.where(i >= 1, ((i - 1) * tq) // tk + 1, 0)

    kidx0 = lax.broadcasted_iota(I32, (tk, tq), 0)
    qpos = q0 + lax.broadcasted_iota(I32, (tk, tq), 1)

    def tile_b(v):
        return jnp.broadcast_to(v, (tk, tq))

    @pl.when(i == 0)
    def _():
        kv_ref[...] = jnp.zeros(kv_ref.shape, BF16)
        ikk_ref[...] = jnp.zeros(ikk_ref.shape, BF16)
        keys_ref[...] = jnp.full(keys_ref.shape, INT_MIN, I32)
        sel_ref[...] = jnp.zeros(sel_ref.shape, F32)
        onehot = (lax.broadcasted_iota(I32, (tq, tq), 0) == lax.broadcasted_iota(I32, (tq, tq), 1))
        for s in range(2):
            for h in range(DSA_HEADS):
                q2_ref[s, h * tq:(h + 1) * tq, dl:2 * dl] = jnp.where(onehot, 1.0, 0.0).astype(BF16)

    @pl.when(has_b)
    def _():
        c = ckv_ref[...]
        cn = c * lax.rsqrt(jnp.mean(c * c, axis=-1, keepdims=True) + LN_EPS) * gck_ref[...]
        kv_ref[pl.ds(q0, tq), 0:dl] = cn.astype(BF16)
        kv_ref[pl.ds(q0, tq), dl:2 * dl] = jnp.ones((tq, dl), BF16)
        ikk_ref[pl.ds(q0, tq), :] = ik_ref[...].astype(BF16)

        qn2 = jnp.zeros((tq, 1), F32)
        for h in range(DSA_HEADS):
            qh = dq_ref[:, h * dl:(h + 1) * dl] * (dl ** -0.5)
            q2_ref[slot_b, h * tq:(h + 1) * tq, 0:dl] = (qh * LOG2_E).astype(BF16)
            qn2 = jnp.maximum(qn2, jnp.sum(qh * qh, axis=1, keepdims=True))
        k_bound = (dl ** 0.5) * SHIFT_SLACK * jnp.max(jnp.abs(gck_ref[...]))
        qn2_row = jnp.broadcast_to(qn2, (tq, LANES)).T[0:1, :]
        sel_ref[slot_b, 2:3, :] = jnp.sqrt(qn2_row) * k_bound

        for h in range(IDX_HEADS):
            iqh = jnp.concatenate(
                [iq_ref[:, h * IDX_DIM:(h + 1) * IDX_DIM], jnp.zeros((tq, LANES - IDX_DIM), F32)], axis=1)
            iq2t_ref[:, h * tq:(h + 1) * tq] = iqh.T.astype(BF16)
        sm_t = small_ref[...].T
        iw_scale = (IDX_HEADS ** -0.5) * (IDX_DIM ** -0.5)
        w_rows = [sm_t[SMALL_IW + h:SMALL_IW + h + 1, :] * iw_scale for h in range(IDX_HEADS)]

        def score_keys(kt):
            k0 = pl.multiple_of(kt * tk, tk)
            s_all = jnp.dot(ikk_ref[pl.ds(k0, tk), :], iq2t_ref[...], preferred_element_type=F32)
            sc = jnp.zeros((tk, tq), F32)
            for h in range(IDX_HEADS):
                sc = sc + w_rows[h] * jnp.maximum(s_all[:, h * tq:(h + 1) * tq], 0.0)
            bits = lax.bitcast_convert_type(sc, I32)
            key = jnp.where(bits < 0, bits ^ MASK_LOW31, bits)
            keys_ref[slot_b * nt + kt] = jnp.where(k0 + kidx0 <= qpos, key, INT_MIN)

        def score_pair(j, carry):
            score_keys(2 * j)
            score_keys(2 * j + 1)
            return carry

        lax.fori_loop(0, (n_b + 1) // 2, score_pair, 0)

    n_acc = 4
    acc_rows = n_acc * SUBLANES

    def b_keys(kt):
        return keys_ref[slot_b * nt + kt]

    def partial_count(hit):
        return jnp.sum(hit.reshape(tk // acc_rows, acc_rows, tq), axis=0)

    def visit(state):
        thr, cnt_acc, p, kt = state
        cand = thr + lax.shift_left(jnp.int32(1), jnp.maximum(SEARCH_BITS - 1 - p, 0))
        cand_t = tile_b(cand)
        for j in range(VISIT_TILES):
            cnt_acc = cnt_acc + partial_count(jnp.where(b_keys(kt + j) >= cand_t, 1.0, 0.0))
        last = kt + VISIT_TILES >= n_b
        cnt = jnp.sum(cnt_acc, axis=0, keepdims=True)
        take = jnp.logical_and(jnp.logical_and(last, p < SEARCH_BITS), cnt >= topk)
        thr = jnp.where(take, cand, thr)
        cnt_acc = jnp.where(last, 0.0, cnt_acc)
        return thr, cnt_acc, jnp.where(last, p + 1, p), jnp.where(last, 0, kt + VISIT_TILES)

    search0 = (jnp.full((1, tq), INT_MIN, I32), jnp.zeros((acc_rows, tq), F32), jnp.int32(0), jnp.int32(0))

    thr_a = tile_b(lax.bitcast_convert_type(sel_ref[slot_a, 0:1, :], I32))
    cut_a = tile_b(sel_ref[slot_a, 1:2, :].astype(I32))
    shift_a = sel_ref[slot_a, 2:3, :]
    acc_ref[...] = jnp.zeros(acc_ref.shape, F32)

    def selected_a(kt):
        key = keys_ref[slot_a * nt + kt]
        return jnp.logical_or(key > thr_a, jnp.logical_and(key == thr_a, kt * tk + kidx0 <= cut_a))

    def att_bounded(state):
        neg_shift = tile_b(-shift_a * LOG2_E)

        def tile(kt, st):
            k0 = pl.multiple_of(kt * tk, tk)
            bias_t = jnp.where(selected_a(kt), neg_shift, MASKED_LOGIT).astype(BF16)
            rhs = jnp.concatenate([kv_ref[pl.ds(k0, tk), 0:dl], bias_t], axis=1)
            kvt = kv_ref[pl.ds(k0, tk), :]
            rows = DSA_HEADS * tq // ATT_CHUNKS
            for c in range(ATT_CHUNKS):
                rs = slice(c * rows, (c + 1) * rows)
                logits = lax.dot_general(q2_ref[slot_a, rs, :], rhs, NT_DIMS, preferred_element_type=F32)
                p = jnp.exp2(logits).astype(BF16)
                acc_ref[rs, :] += jnp.dot(p, kvt, preferred_element_type=F32)
                for _ in range(VISITS_PER_TILE // ATT_CHUNKS):
                    st = visit(st)
            return st

        return lax.fori_loop(0, n_a, tile, state)

    def att_online(state):
        m_ref[...] = jnp.full(m_ref.shape, MASKED_LOGIT, F32)

        def tile(kt, carry):
            k0 = pl.multiple_of(kt * tk, tk)
            kvt = kv_ref[pl.ds(k0, tk), :]
            bias = jnp.where(selected_a(kt), 0.0, -jnp.inf).T
            logits = lax.dot_general(q2_ref[slot_a, :, 0:dl], kvt[:, 0:dl], NT_DIMS,
                                     preferred_element_type=F32) * (1.0 / LOG2_E)
            logits = (logits.reshape(DSA_HEADS, tq, tk) + bias[None]).reshape(DSA_HEADS * tq, tk)
            m_old = m_ref[...]
            m_new = jnp.maximum(m_old, jnp.max(logits, axis=1, keepdims=True))
            alpha = jnp.exp(m_old - m_new)
            p = jnp.exp(logits - m_new).astype(BF16)
            acc_ref[...] = acc_ref[...] * alpha + jnp.dot(p, kvt, preferred_element_type=F32)
            m_ref[...] = m_new
            return carry

        lax.fori_loop(0, n_a, tile, 0)
        return state

    state = lax.cond(jnp.max(shift_a) <= MAX_SAFE_SHIFT, att_bounded, att_online, search0)

    def visits_left(st):
        return jnp.logical_and(st[2] < SEARCH_BITS, has_b)

    thr, _, _, _ = lax.while_loop(visits_left, visit, state)
    thr_t = tile_b(thr)

    def count(pred):
        def body(kt, acc):
            return acc + partial_count(jnp.where(pred(b_keys(kt), kt * tk + kidx0), 1.0, 0.0))
        acc = lax.fori_loop(0, jnp.where(has_b, n_b, 0), body, jnp.zeros((acc_rows, tq), F32))
        return jnp.sum(acc, axis=0, keepdims=True)

    n_gt = count(lambda k, s: k > thr_t)
    n_ge = count(lambda k, s: k >= thr_t)
    need = topk - n_gt
    is_min = thr == INT_MIN
    partial = jnp.logical_and(n_ge - n_gt > need, jnp.logical_not(is_min))
    any_partial = jnp.max(jnp.where(partial, 1.0, 0.0)) > 0.0
    nbits = max(1, (L - 1).bit_length())

    def tie_search():
        def idx_pass(p, d):
            cand = d + lax.shift_left(jnp.int32(1), nbits - 1 - p)
            cb = tile_b(cand)
            cnt = count(lambda k, s: jnp.logical_and(k == thr_t, s < cb))
            return jnp.where(cnt < need, cand, d)
        return lax.fori_loop(0, nbits, idx_pass, jnp.zeros((1, tq), I32))

    cut = lax.cond(any_partial, tie_search, lambda: jnp.zeros((1, tq), I32))
    cut = jnp.where(partial, cut, L)
    cut = jnp.where(is_min, -1, cut)
    sel_ref[slot_b, 0:1, :] = lax.bitcast_convert_type(thr, F32)
    sel_ref[slot_b, 1:2, :] = cut.astype(F32)

    @pl.when(i >= 1)
    def _():
        acc = acc_ref[...]
        o = acc[:, 0:dl] * (1.0 / acc[:, dl:2 * dl])
        for h in range(DSA_HEADS):
            out_ref[:, h * dl:(h + 1) * dl] = o[h * tq:(h + 1) * tq, :].astype(BF16)


def _dsa(proj, B, L, gck):
    T = B * L
    tq = 128
    tk = min(512, L)
    nq = L // tq
    topk = min(TOPK_MAX, L // 4)
    assert (L // tk) % 2 == 0 and tk >= topk and tq == LANES, "score tiles are visited in pairs"

    def blk(width, col):
        cb = col // width
        return pl.BlockSpec((tq, width), lambda b, i: (b * nq + jnp.minimum(i, nq - 1), cb))

    def full(shape):
        return pl.BlockSpec(shape, lambda b, i: (0,) * len(shape))

    return pl.pallas_call(
        functools.partial(_dsa_body, L=L, tq=tq, tk=tk, topk=topk),
        grid=(B, nq + 1),
        in_specs=[blk(2048, COL_DQ), blk(512, COL_IQ), blk(128, COL_CKV), blk(128, COL_IK),
                  blk(128, COL_SMALL), full(gck.shape)],
        out_specs=pl.BlockSpec((tq, DSA_HEADS * DSA_LATENT), lambda b, i: (b * nq + jnp.maximum(i - 1, 0), 0)),
        out_shape=jax.ShapeDtypeStruct((T, DSA_HEADS * DSA_LATENT), BF16),
        scratch_shapes=[pltpu.VMEM((L, 2 * DSA_LATENT), BF16),
                        pltpu.VMEM((L, LANES), BF16),
                        pltpu.VMEM((2 * (L // tk), tk, tq), I32),
                        pltpu.VMEM((2, DSA_HEADS * tq, 2 * DSA_LATENT), BF16),
                        pltpu.VMEM((LANES, IDX_HEADS * tq), BF16),
                        pltpu.VMEM((2, SUBLANES, tq), F32),
                        pltpu.VMEM((DSA_HEADS * tq, 2 * DSA_LATENT), F32),
                        pltpu.VMEM((DSA_HEADS * tq, 1), F32)],
        compiler_params=pltpu.CompilerParams(
            dimension_semantics=("parallel", "arbitrary"),
            vmem_limit_bytes=56 * 1024 * 1024),
        name="dsa",
    )(proj, proj, proj, proj, proj, gck)


def _merge_body(a_ref, ob_ref, gate_ref, x_ref, wuv_ref, wdsa_ref, w_ref, g_ref, be_ref, o_ref, *, alpha):
    ob = ob_ref[...]
    pw = 2 * DSA_LATENT
    u = jnp.concatenate(
        [jnp.dot(ob[:, j * pw:(j + 1) * pw], wuv_ref[j], preferred_element_type=F32).astype(BF16)
         for j in range(DSA_HEADS // 2)], axis=1)
    yb = jnp.dot(u, wdsa_ref[...], preferred_element_type=F32)
    s = (a_ref[...] + _sigmoid(gate_ref[...]) * yb).astype(BF16)
    mixed = jnp.dot(s, w_ref[...], preferred_element_type=F32)
    o_ref[...] = _layer_norm(alpha * x_ref[...] + mixed, g_ref[...], be_ref[...])


def _merge(ya, ob, proj, x2d, wuv_pairs, wdsa, w_out, g, b, alpha):
    T = x2d.shape[0]
    tm = min(512, T)
    row = pl.BlockSpec((tm, D_MODEL), lambda i: (i, 0))
    gate_cb = COL_GATE_B // D_MODEL

    def full(shape):
        return pl.BlockSpec(shape, lambda i: (0,) * len(shape))

    return pl.pallas_call(
        functools.partial(_merge_body, alpha=alpha),
        grid=(T // tm,),
        in_specs=[row, pl.BlockSpec((tm, DSA_HEADS * DSA_LATENT), lambda i: (i, 0)),
                  pl.BlockSpec((tm, D_MODEL), lambda i: (i, gate_cb)), row,
                  full(wuv_pairs.shape), full(wdsa.shape), full(w_out.shape), full(g.shape), full(b.shape)],
        out_specs=row,
        out_shape=jax.ShapeDtypeStruct((T, D_MODEL), F32),
        compiler_params=pltpu.CompilerParams(
            dimension_semantics=("parallel",), vmem_limit_bytes=40 * 1024 * 1024),
        name="merge",
    )(ya, ob, proj, x2d, wuv_pairs, wdsa, w_out, g, b)


FFN_HALO = 16
FFN_CHUNK = 256


def _ffn_body(xm_ref, xh_ref, wup_ref, cw_ref, cb_ref, wd_ref, ln2g_ref, ln2b_ref, p_ref, wple_ref,
              wpg_ref, ln3g_ref, ln3b_ref, o_ref, act_ref, h_ref, *, tm, L, alpha):
    i = pl.program_id(0)
    at_start = (i * tm) % L == 0
    halo = jnp.where(at_start, 0.0, xh_ref[...])
    xb = jnp.concatenate([halo, xm_ref[...]], axis=0).astype(BF16)

    def conv_branch(col, slot):
        cs = slice(col, col + FFN_CHUNK)
        h_ref[slot] = jnp.dot(xb, wup_ref[:, cs], preferred_element_type=F32)
        hs = h_ref.at[slot]
        return (cw_ref[2:3, cs] * hs[FFN_HALO:FFN_HALO + tm, :]
                + cw_ref[0:1, cs] * hs[FFN_HALO - 2:FFN_HALO - 2 + tm, :]
                + cw_ref[1:2, cs] * hs[FFN_HALO - 1:FFN_HALO - 1 + tm, :]
                + cb_ref[:, cs])

    for c in range(D_FF // FFN_CHUNK):
        hg = conv_branch(c * FFN_CHUNK, (2 * c) % 4)
        hv = conv_branch(D_FF + c * FFN_CHUNK, (2 * c + 1) % 4)
        act_ref[:, c * FFN_CHUNK:(c + 1) * FFN_CHUNK] = ((hg * _sigmoid(hg)) * hv).astype(BF16)

    ffn = jnp.dot(act_ref[...], wd_ref[...], preferred_element_type=F32)
    x2 = _layer_norm(alpha * xm_ref[...] + ffn, ln2g_ref[...], ln2b_ref[...])
    gate = _sigmoid(jnp.dot(x2.astype(BF16), wpg_ref[...], preferred_element_type=F32))
    ple = gate * jnp.dot(p_ref[...].astype(BF16), wple_ref[...], preferred_element_type=F32)
    o_ref[...] = _layer_norm(alpha * x2 + ple, ln3g_ref[...], ln3b_ref[...])


def _ffn(x1, L, w_up, conv_w, conv_b, w_down, ln2g, ln2b, p2d, w_ple, w_pg, ln3g, ln3b, alpha):
    T = x1.shape[0]
    tm = min(512, L)
    hb = tm // FFN_HALO

    def full(shape):
        return pl.BlockSpec(shape, lambda i: (0,) * len(shape), pipeline_mode=pl.Buffered(1))

    return pl.pallas_call(
        functools.partial(_ffn_body, tm=tm, L=L, alpha=alpha),
        grid=(T // tm,),
        in_specs=[pl.BlockSpec((tm, D_MODEL), lambda i: (i, 0)),
                  pl.BlockSpec((FFN_HALO, D_MODEL), lambda i: (jnp.maximum(i * hb - 1, 0), 0)),
                  full(w_up.shape), full(conv_w.shape), full(conv_b.shape), full(w_down.shape),
                  full(ln2g.shape), full(ln2b.shape),
                  pl.BlockSpec((tm, PLE_DIM), lambda i: (i, 0)),
                  full(w_ple.shape), full(w_pg.shape), full(ln3g.shape), full(ln3b.shape)],
        out_specs=pl.BlockSpec((tm, D_MODEL), lambda i: (i, 0)),
        out_shape=jax.ShapeDtypeStruct((T, D_MODEL), F32),
        scratch_shapes=[pltpu.VMEM((tm, D_FF), BF16),
                        pltpu.VMEM((4, tm + FFN_HALO, FFN_CHUNK), F32)],
        compiler_params=pltpu.CompilerParams(
            dimension_semantics=("parallel",),
            vmem_limit_bytes=52 * 1024 * 1024),
        name="ffn",
    )(x1, x1, w_up, conv_w, conv_b, w_down, ln2g, ln2b, p2d, w_ple, w_pg, ln3g, ln3b)


def _permute_w_in(w):
    D = w.shape[0]
    o = 0
    seg = {}
    for name, width in (("gq", 512), ("gk", 512), ("gv", 1024), ("gr", 1024), ("ga", GLA_GATE_RANK),
                        ("dq", 2048), ("ckv", 128), ("iq", 512), ("ik", 64), ("iw", 8),
                        ("gate_a", 1024), ("gate_b", 1024)):
        seg[name] = w[:, o:o + width]
        o += width
    z = lambda n: jnp.zeros((D, n), w.dtype)
    small = jnp.concatenate([seg["ga"], seg["iw"], z(LANES - GLA_GATE_RANK - IDX_HEADS)], axis=1)
    parts = [seg["dq"], seg["gate_a"], seg["gate_b"], seg["gv"], seg["gr"], seg["gq"], seg["gk"], seg["iq"],
             seg["ckv"], jnp.concatenate([seg["ik"], z(LANES - IDX_DIM)], axis=1), small]
    used = sum(p.shape[1] for p in parts)
    parts.append(z(PROJ_COLS - used))
    return jnp.concatenate(parts, axis=1).astype(BF16)


def _pair_block_diag(w_uv):
    H, dc, dh = w_uv.shape
    z = jnp.zeros((dc, dh), w_uv.dtype)
    blocks = [jnp.concatenate([jnp.concatenate([w_uv[2 * j], z], axis=1),
                               jnp.concatenate([z, w_uv[2 * j + 1]], axis=1)], axis=0)
              for j in range(H // 2)]
    return jnp.stack(blocks).astype(BF16)


def kernel(x, p, w_in, w_gla_gate_up, b_gla_gate, g_gla_norm, w_gla_proj, g_ckv_norm, w_uv, w_dsa_proj,
           w_out, ln1_g, ln1_b, w_up, conv_w, conv_b, w_down, ln2_g, ln2_b, w_ple, w_ple_gate, ln3_g,
           ln3_b):
    B, L, D = x.shape
    depth = w_in.shape[0]
    alpha = (2.0 * depth) ** 0.25
    T = B * L
    x2d = x.reshape(T, D)
    row = lambda v: v.reshape(1, -1)
    for i in range(depth):
        w_pad = _permute_w_in(w_in[i])
        proj = _proj(x2d, w_pad)
        wup_pad = jnp.concatenate(
            [w_gla_gate_up[i], jnp.zeros((LANES - GLA_GATE_RANK, GLA_HEADS * GLA_DK), F32)], axis=0).astype(BF16)
        ya = _gla(proj, B, L, wup_pad, row(b_gla_gate[i]), row(g_gla_norm[i]), w_gla_proj[i].astype(BF16))
        ob = _dsa(proj, B, L, row(g_ckv_norm[i]))
        x1 = _merge(ya, ob, proj, x2d, _pair_block_diag(w_uv[i]), w_dsa_proj[i].astype(BF16),
                    w_out[i].astype(BF16), row(ln1_g[i]), row(ln1_b[i]), alpha)
        x2d = _ffn(x1, L, w_up[i].astype(BF16), conv_w[i], row(conv_b[i]), w_down[i].astype(BF16),
                   row(ln2_g[i]), row(ln2_b[i]), p[i].reshape(T, PLE_DIM), w_ple[i].astype(BF16),
                   w_ple_gate[i].astype(BF16), row(ln3_g[i]), row(ln3_b[i]), alpha)
    return x2d.reshape(B, L, D)
```

```python
import functools

import jax
import jax.numpy as jnp
from jax import lax
from jax.experimental import pallas as pl
from jax.experimental.pallas import tpu as pltpu

F32 = jnp.float32
BF16 = jnp.bfloat16
I32 = jnp.int32

D_MODEL = 1024
GLA_HEADS = 4
GLA_DK = 128
GLA_DV = 256
GLA_GATE_RANK = 16
GLA_GATE_TEMP = 16.0
GLA_CHUNK = 64
DSA_HEADS = 16
DSA_HEAD_DIM = 64
DSA_LATENT = 128
IDX_HEADS = 8
IDX_DIM = 64
TOPK_MAX = 256
D_FF = 2816
CONV_W = 3
LN_EPS = 1e-5
PLE_DIM = 256

LANES = 128
SUBLANES = 8
INT_MIN = -2147483648
MASK_LOW31 = 0x7FFFFFFF
MASKED_LOGIT = -1e30
MAX_SAFE_SHIFT = 40.0
SHIFT_SLACK = 1.02
LOG2_E = 1.4426950408889634

PROJ_COLS = 8192
COL_DQ = 0
COL_GATE_A = 2048
COL_GATE_B = 3072
COL_GV = 4096
COL_GR = 5120
COL_GQ = 6144
COL_GK = 6656
COL_IQ = 7168
COL_CKV = 7680
COL_IK = 7808
COL_SMALL = 7936
SMALL_IW = GLA_GATE_RANK

MIB = 1024 * 1024
PROJ_TM, PROJ_TN = 2048, 1024
GLA_TT = 256
DSA_TQ, DSA_TK = 128, 512
ROW_TM = 512
VMEM_SMALL = 40 * MIB
VMEM_FFN = 52 * MIB
VMEM_DSA = 56 * MIB

NT_DIMS = (((1,), (1,)), ((), ()))
TN_DIMS = (((0,), (0,)), ((), ()))


def _layer_norm(v, g, b):
    mu = jnp.mean(v, axis=-1, keepdims=True)
    d = v - mu
    var = jnp.mean(d * d, axis=-1, keepdims=True)
    return d * lax.rsqrt(var + LN_EPS) * g + b


def _sigmoid(v):
    return 1.0 / (1.0 + jnp.exp(-v))


def _proj_body(x_ref, w_ref, o_ref):
    o_ref[...] = jnp.dot(x_ref[...].astype(BF16), w_ref[...], preferred_element_type=F32)


def _proj(x2d, w_pad):
    T, D = x2d.shape
    N = w_pad.shape[1]
    tm = min(PROJ_TM, T)
    tn = PROJ_TN
    return pl.pallas_call(
        _proj_body,
        grid=(T // tm, N // tn),
        in_specs=[pl.BlockSpec((tm, D), lambda i, j: (i, 0)),
                  pl.BlockSpec((D, tn), lambda i, j: (0, j))],
        out_specs=pl.BlockSpec((tm, tn), lambda i, j: (i, j)),
        out_shape=jax.ShapeDtypeStruct((T, N), F32),
        compiler_params=pltpu.CompilerParams(
            dimension_semantics=("parallel", "arbitrary"),
            vmem_limit_bytes=VMEM_FFN),
        name="proj",
    )(x2d, w_pad)


def _gla_body(gq_ref, gk_ref, gv_ref, gr_ref, small_ref, gate_ref, wup_ref, bup_ref, gnorm_ref,
              wproj_ref, out_ref, st_ref, oa_ref, *, tt):
    @pl.when(pl.program_id(1) == 0)
    def _():
        st_ref[...] = jnp.zeros(st_ref.shape, F32)

    C = GLA_CHUNK
    nc = tt // C
    row = lax.broadcasted_iota(I32, (tt, tt), 0)
    col = lax.broadcasted_iota(I32, (tt, tt), 1)
    tril = jnp.logical_and(row >= col, row // C == col // C)
    tril_f = tril.astype(F32)
    z = jnp.dot(small_ref[...].astype(BF16), wup_ref[...], preferred_element_type=F32) + bup_ref[...]
    log_a = (jnp.minimum(z, 0.0) - jnp.log(1.0 + jnp.exp(-jnp.abs(z)))) * (1.0 / GLA_GATE_TEMP)
    bc = jnp.dot(tril_f, log_a, preferred_element_type=F32, precision=lax.Precision.HIGHEST)
    bl = jnp.concatenate(
        [jnp.broadcast_to(bc[(c + 1) * C - 1:(c + 1) * C, :], (C, bc.shape[1])) for c in range(nc)], axis=0)
    e_b = jnp.exp(bc)
    e_nb = jnp.exp(-bc)
    e_bl = jnp.exp(bl - bc)
    for h in range(GLA_HEADS):
        ks = slice(h * GLA_DK, (h + 1) * GLA_DK)
        vs = slice(h * GLA_DV, (h + 1) * GLA_DV)
        q_in = (gq_ref[:, ks] * (GLA_DK ** -0.5) * e_b[:, ks]).astype(BF16)
        k = gk_ref[:, ks]
        k_in = (k * e_nb[:, ks]).astype(BF16)
        k_st = (k * e_bl[:, ks]).astype(BF16)
        v = gv_ref[:, vs].astype(BF16)
        att = lax.dot_general(q_in, k_in, NT_DIMS, preferred_element_type=F32)
        att = jnp.where(tril, att, 0.0).astype(BF16)
        o_intra = jnp.dot(att, v, preferred_element_type=F32)
        st = st_ref[h]
        o_inter = []
        for c in range(nc):
            r = slice(c * C, (c + 1) * C)
            o_inter.append(lax.dot_general(q_in[r], st.astype(BF16), NT_DIMS, preferred_element_type=F32))
            dec = jnp.exp(bc[(c + 1) * C - 1:(c + 1) * C, ks])
            st = st * dec + lax.dot_general(v[r], k_st[r], TN_DIMS, preferred_element_type=F32)
        st_ref[h] = st
        o = o_intra + jnp.concatenate(o_inter, axis=0)
        ms = jnp.mean(o * o, axis=-1, keepdims=True)
        on = o * lax.rsqrt(ms + LN_EPS) * gnorm_ref[:, vs]
        g = gr_ref[:, vs]
        oa_ref[:, vs] = (on * (g * _sigmoid(g))).astype(BF16)
    y = jnp.dot(oa_ref[...], wproj_ref[...], preferred_element_type=F32)
    out_ref[...] = _sigmoid(gate_ref[...]) * y


def _gla(proj, B, L, wup_pad, bup, gnorm, wproj):
    T = B * L
    tt = GLA_TT
    nt = L // tt

    def blk(width, col):
        cb = col // width
        return pl.BlockSpec((tt, width), lambda b, i: (b * nt + i, cb))

    def full(shape):
        return pl.BlockSpec(shape, lambda b, i: (0,) * len(shape))

    return pl.pallas_call(
        functools.partial(_gla_body, tt=tt),
        grid=(B, nt),
        in_specs=[blk(512, COL_GQ), blk(512, COL_GK), blk(1024, COL_GV), blk(1024, COL_GR),
                  blk(128, COL_SMALL), blk(1024, COL_GATE_A),
                  full(wup_pad.shape), full(bup.shape), full(gnorm.shape), full(wproj.shape)],
        out_specs=pl.BlockSpec((tt, D_MODEL), lambda b, i: (b * nt + i, 0)),
        out_shape=jax.ShapeDtypeStruct((T, D_MODEL), F32),
        scratch_shapes=[pltpu.VMEM((GLA_HEADS, GLA_DV, GLA_DK), F32),
                        pltpu.VMEM((tt, GLA_HEADS * GLA_DV), BF16)],
        compiler_params=pltpu.CompilerParams(
            dimension_semantics=("parallel", "arbitrary"),
            vmem_limit_bytes=VMEM_SMALL),
        name="gla",
    )(proj, proj, proj, proj, proj, proj, wup_pad, bup, gnorm, wproj)


SEARCH_BITS = 32
VISITS_PER_TILE = 24
ATT_CHUNKS = 8


def _dsa_body(dq_ref, iq_ref, ckv_ref, ik_ref, small_ref, gck_ref,
              out_ref, kv_ref, ikk_ref, keys_ref, q2_ref, iq2t_ref, sel_ref, acc_ref, m_ref,
              *, L, tq, tk, topk):
    i = pl.program_id(1)
    nq = L // tq
    nt = L // tk
    dl = DSA_LATENT
    has_b = i < nq
    slot_b = i % 2
    slot_a = 1 - slot_b
    qb = jnp.minimum(i, nq - 1)
    q0 = pl.multiple_of(qb * tq, tq)
    n_b = q0 // tk + 1
    n_a = jnp.where(i >= 1, ((i - 1) * tq) // tk + 1, 0)

    kidx0 = lax.broadcasted_iota(I32, (tk, tq), 0)
    qpos = q0 + lax.broadcasted_iota(I32, (tk, tq), 1)

    def tile_b(v):
        return jnp.broadcast_to(v, (tk, tq))

    @pl.when(i == 0)
    def _():
        kv_ref[...] = jnp.zeros(kv_ref.shape, BF16)
        ikk_ref[...] = jnp.zeros(ikk_ref.shape, BF16)
        keys_ref[...] = jnp.full(keys_ref.shape, INT_MIN, I32)
        sel_ref[...] = jnp.zeros(sel_ref.shape, F32)
        onehot = (lax.broadcasted_iota(I32, (tq, tq), 0) == lax.broadcasted_iota(I32, (tq, tq), 1))
        for s in range(2):
            for h in range(DSA_HEADS):
                q2_ref[s, h * tq:(h + 1) * tq, dl:2 * dl] = jnp.where(onehot, 1.0, 0.0).astype(BF16)

    @pl.when(has_b)
    def _():
        c = ckv_ref[...]
        cn = c * lax.rsqrt(jnp.mean(c * c, axis=-1, keepdims=True) + LN_EPS) * gck_ref[...]
        kv_ref[pl.ds(q0, tq), 0:dl] = cn.astype(BF16)
        kv_ref[pl.ds(q0, tq), dl:2 * dl] = jnp.ones((tq, dl), BF16)
        ikk_ref[pl.ds(q0, tq), :] = ik_ref[...].astype(BF16)

        qn2 = jnp.zeros((tq, 1), F32)
        for h in range(DSA_HEADS):
            qh = dq_ref[:, h * dl:(h + 1) * dl] * (dl ** -0.5)
            q2_ref[slot_b, h * tq:(h + 1) * tq, 0:dl] = (qh * LOG2_E).astype(BF16)
            qn2 = jnp.maximum(qn2, jnp.sum(qh * qh, axis=1, keepdims=True))
        k_bound = (dl ** 0.5) * SHIFT_SLACK * jnp.max(jnp.abs(gck_ref[...]))
        qn2_row = jnp.broadcast_to(qn2, (tq, LANES)).T[0:1, :]
        sel_ref[slot_b, 2:3, :] = jnp.sqrt(qn2_row) * k_bound

        for h in range(IDX_HEADS):
            iqh = jnp.concatenate(
                [iq_ref[:, h * IDX_DIM:(h + 1) * IDX_DIM], jnp.zeros((tq, LANES - IDX_DIM), F32)], axis=1)
            iq2t_ref[:, h * tq:(h + 1) * tq] = iqh.T.astype(BF16)
        sm_t = small_ref[...].T
        iw_scale = (IDX_HEADS ** -0.5) * (IDX_DIM ** -0.5)
        w_rows = [sm_t[SMALL_IW + h:SMALL_IW + h + 1, :] * iw_scale for h in range(IDX_HEADS)]

        def score_keys(kt):
            k0 = pl.multiple_of(kt * tk, tk)
            s_all = jnp.dot(ikk_ref[pl.ds(k0, tk), :], iq2t_ref[...], preferred_element_type=F32)
            sc = jnp.zeros((tk, tq), F32)
            for h in range(IDX_HEADS):
                sc = sc + w_rows[h] * jnp.maximum(s_all[:, h * tq:(h + 1) * tq], 0.0)
            bits = lax.bitcast_convert_type(sc, I32)
            key = jnp.where(bits < 0, bits ^ MASK_LOW31, bits)
            keys_ref[slot_b * nt + kt] = jnp.where(k0 + kidx0 <= qpos, key, INT_MIN)

        def score_pair(j, carry):
            score_keys(2 * j)
            score_keys(2 * j + 1)
            return carry

        lax.fori_loop(0, (n_b + 1) // 2, score_pair, 0)

    n_acc = 4
    acc_rows = n_acc * SUBLANES

    def b_keys(kt):
        return keys_ref[slot_b * nt + kt]

    def partial_count(hit):
        return jnp.sum(hit.reshape(tk // acc_rows, acc_rows, tq), axis=0)

    def visit(state):
        thr, cnt_acc, open_q, p, kt = state
        cand = thr + lax.shift_left(jnp.int32(1), jnp.maximum(SEARCH_BITS - 1 - p, 0))
        cnt_acc = cnt_acc + partial_count(jnp.where(b_keys(kt) >= tile_b(cand), 1.0, 0.0))
        last = kt == n_b - 1
        cnt = jnp.sum(cnt_acc, axis=0, keepdims=True)
        take = jnp.logical_and(jnp.logical_and(last, p < SEARCH_BITS), cnt >= topk)
        thr = jnp.where(take, cand, thr)
        open_q = jnp.where(jnp.logical_and(take, cnt == topk), 0.0, open_q)
        cnt_acc = jnp.where(last, 0.0, cnt_acc)
        return thr, cnt_acc, open_q, jnp.where(last, p + 1, p), jnp.where(last, 0, kt + 1)

    open0 = jnp.where(qpos[0:1, :] + 1 >= topk, 1.0, 0.0)
    search0 = (jnp.full((1, tq), INT_MIN, I32), jnp.zeros((acc_rows, tq), F32), open0,
               jnp.int32(0), jnp.int32(0))

    thr_a = tile_b(lax.bitcast_convert_type(sel_ref[slot_a, 0:1, :], I32))
    cut_a = tile_b(sel_ref[slot_a, 1:2, :].astype(I32))
    shift_a = sel_ref[slot_a, 2:3, :]
    acc_ref[...] = jnp.zeros(acc_ref.shape, F32)

    def selected_a(kt):
        key = keys_ref[slot_a * nt + kt]
        return jnp.logical_or(key > thr_a, jnp.logical_and(key == thr_a, kt * tk + kidx0 <= cut_a))

    def att_bounded(state):
        neg_shift = tile_b(-shift_a * LOG2_E)

        def tile(kt, st):
            k0 = pl.multiple_of(kt * tk, tk)
            bias_t = jnp.where(selected_a(kt), neg_shift, MASKED_LOGIT).astype(BF16)
            rhs = jnp.concatenate([kv_ref[pl.ds(k0, tk), 0:dl], bias_t], axis=1)
            kvt = kv_ref[pl.ds(k0, tk), :]
            rows = DSA_HEADS * tq // ATT_CHUNKS
            for c in range(ATT_CHUNKS):
                rs = slice(c * rows, (c + 1) * rows)
                logits = lax.dot_general(q2_ref[slot_a, rs, :], rhs, NT_DIMS, preferred_element_type=F32)
                p = jnp.exp2(logits).astype(BF16)
                acc_ref[rs, :] += jnp.dot(p, kvt, preferred_element_type=F32)
                for _ in range(VISITS_PER_TILE // ATT_CHUNKS):
                    st = visit(st)
            return st

        return lax.fori_loop(0, n_a, tile, state)

    def att_online(state):
        m_ref[...] = jnp.full(m_ref.shape, MASKED_LOGIT, F32)

        def tile(kt, carry):
            k0 = pl.multiple_of(kt * tk, tk)
            kvt = kv_ref[pl.ds(k0, tk), :]
            bias = jnp.where(selected_a(kt), 0.0, -jnp.inf).T
            logits = lax.dot_general(q2_ref[slot_a, :, 0:dl], kvt[:, 0:dl], NT_DIMS,
                                     preferred_element_type=F32) * (1.0 / LOG2_E)
            logits = (logits.reshape(DSA_HEADS, tq, tk) + bias[None]).reshape(DSA_HEADS * tq, tk)
            m_old = m_ref[...]
            m_new = jnp.maximum(m_old, jnp.max(logits, axis=1, keepdims=True))
            alpha = jnp.exp(m_old - m_new)
            p = jnp.exp(logits - m_new).astype(BF16)
            acc_ref[...] = acc_ref[...] * alpha + jnp.dot(p, kvt, preferred_element_type=F32)
            m_ref[...] = m_new
            return carry

        lax.fori_loop(0, n_a, tile, 0)
        return state

    state = lax.cond(jnp.max(shift_a) <= MAX_SAFE_SHIFT, att_bounded, att_online, search0)

    def in_pass(st):
        return jnp.logical_and(jnp.logical_and(st[4] > 0, st[3] < SEARCH_BITS), has_b)

    state = lax.while_loop(in_pass, visit, state)

    def passes_left(st):
        return jnp.logical_and(jnp.logical_and(st[3] < SEARCH_BITS, has_b), jnp.max(st[2]) > 0.0)

    def one_pass(st):
        return lax.fori_loop(0, n_b, lambda _, s: visit(s), st)

    thr = lax.while_loop(passes_left, one_pass, state)[0]
    thr_t = tile_b(thr)

    def count(pred):
        def body(kt, acc):
            return acc + partial_count(jnp.where(pred(b_keys(kt), kt * tk + kidx0), 1.0, 0.0))
        acc = lax.fori_loop(0, jnp.where(has_b, n_b, 0), body, jnp.zeros((acc_rows, tq), F32))
        return jnp.sum(acc, axis=0, keepdims=True)

    n_gt = count(lambda k, s: k > thr_t)
    n_ge = count(lambda k, s: k >= thr_t)
    need = topk - n_gt
    is_min = thr == INT_MIN
    partial = jnp.logical_and(n_ge - n_gt > need, jnp.logical_not(is_min))
    any_partial = jnp.max(jnp.where(partial, 1.0, 0.0)) > 0.0
    nbits = max(1, (L - 1).bit_length())

    def tie_search():
        def idx_pass(p, d):
            cand = d + lax.shift_left(jnp.int32(1), nbits - 1 - p)
            cb = tile_b(cand)
            cnt = count(lambda k, s: jnp.logical_and(k == thr_t, s < cb))
            return jnp.where(cnt < need, cand, d)
        return lax.fori_loop(0, nbits, idx_pass, jnp.zeros((1, tq), I32))

    cut = lax.cond(any_partial, tie_search, lambda: jnp.zeros((1, tq), I32))
    cut = jnp.where(partial, cut, L)
    cut = jnp.where(is_min, -1, cut)
    sel_ref[slot_b, 0:1, :] = lax.bitcast_convert_type(thr, F32)
    sel_ref[slot_b, 1:2, :] = cut.astype(F32)

    @pl.when(i >= 1)
    def _():
        acc = acc_ref[...]
        o = acc[:, 0:dl] * (1.0 / acc[:, dl:2 * dl])
        for h in range(DSA_HEADS):
            out_ref[:, h * dl:(h + 1) * dl] = o[h * tq:(h + 1) * tq, :].astype(BF16)


def _dsa(proj, B, L, gck):
    T = B * L
    tq = DSA_TQ
    tk = min(DSA_TK, L)
    nq = L // tq
    topk = min(TOPK_MAX, L // 4)
    assert (L // tk) % 2 == 0 and tk >= topk and tq == LANES, "score tiles are visited in pairs"

    def blk(width, col):
        cb = col // width
        return pl.BlockSpec((tq, width), lambda b, i: (b * nq + jnp.minimum(i, nq - 1), cb))

    def full(shape):
        return pl.BlockSpec(shape, lambda b, i: (0,) * len(shape))

    return pl.pallas_call(
        functools.partial(_dsa_body, L=L, tq=tq, tk=tk, topk=topk),
        grid=(B, nq + 1),
        in_specs=[blk(2048, COL_DQ), blk(512, COL_IQ), blk(128, COL_CKV), blk(128, COL_IK),
                  blk(128, COL_SMALL), full(gck.shape)],
        out_specs=pl.BlockSpec((tq, DSA_HEADS * DSA_LATENT), lambda b, i: (b * nq + jnp.maximum(i - 1, 0), 0)),
        out_shape=jax.ShapeDtypeStruct((T, DSA_HEADS * DSA_LATENT), BF16),
        scratch_shapes=[pltpu.VMEM((L, 2 * DSA_LATENT), BF16),
                        pltpu.VMEM((L, LANES), BF16),
                        pltpu.VMEM((2 * (L // tk), tk, tq), I32),
                        pltpu.VMEM((2, DSA_HEADS * tq, 2 * DSA_LATENT), BF16),
                        pltpu.VMEM((LANES, IDX_HEADS * tq), BF16),
                        pltpu.VMEM((2, SUBLANES, tq), F32),
                        pltpu.VMEM((DSA_HEADS * tq, 2 * DSA_LATENT), F32),
                        pltpu.VMEM((DSA_HEADS * tq, 1), F32)],
        compiler_params=pltpu.CompilerParams(
            dimension_semantics=("parallel", "arbitrary"),
            vmem_limit_bytes=VMEM_DSA),
        name="dsa",
    )(proj, proj, proj, proj, proj, gck)


def _merge_body(a_ref, ob_ref, gate_ref, x_ref, wuv_ref, wdsa_ref, w_ref, g_ref, be_ref, o_ref, *, alpha):
    ob = ob_ref[...]
    pw = 2 * DSA_LATENT
    u = jnp.concatenate(
        [jnp.dot(ob[:, j * pw:(j + 1) * pw], wuv_ref[j], preferred_element_type=F32).astype(BF16)
         for j in range(DSA_HEADS // 2)], axis=1)
    yb = jnp.dot(u, wdsa_ref[...], preferred_element_type=F32)
    s = (a_ref[...] + _sigmoid(gate_ref[...]) * yb).astype(BF16)
    mixed = jnp.dot(s, w_ref[...], preferred_element_type=F32)
    o_ref[...] = _layer_norm(alpha * x_ref[...] + mixed, g_ref[...], be_ref[...])


def _merge(ya, ob, proj, x2d, wuv_pairs, wdsa, w_out, g, b, alpha):
    T = x2d.shape[0]
    tm = min(ROW_TM, T)
    row = pl.BlockSpec((tm, D_MODEL), lambda i: (i, 0))
    gate_cb = COL_GATE_B // D_MODEL

    def full(shape):
        return pl.BlockSpec(shape, lambda i: (0,) * len(shape))

    return pl.pallas_call(
        functools.partial(_merge_body, alpha=alpha),
        grid=(T // tm,),
        in_specs=[row, pl.BlockSpec((tm, DSA_HEADS * DSA_LATENT), lambda i: (i, 0)),
                  pl.BlockSpec((tm, D_MODEL), lambda i: (i, gate_cb)), row,
                  full(wuv_pairs.shape), full(wdsa.shape), full(w_out.shape), full(g.shape), full(b.shape)],
        out_specs=row,
        out_shape=jax.ShapeDtypeStruct((T, D_MODEL), F32),
        compiler_params=pltpu.CompilerParams(
            dimension_semantics=("parallel",), vmem_limit_bytes=VMEM_SMALL),
        name="merge",
    )(ya, ob, proj, x2d, wuv_pairs, wdsa, w_out, g, b)


FFN_HALO = 16
FFN_CHUNK = 256


def _ffn_body(xm_ref, xh_ref, wup_ref, cw_ref, cb_ref, wd_ref, ln2g_ref, ln2b_ref, p_ref, wple_ref,
              wpg_ref, ln3g_ref, ln3b_ref, o_ref, act_ref, h_ref, *, tm, L, alpha):
    i = pl.program_id(0)
    at_start = (i * tm) % L == 0
    halo = jnp.where(at_start, 0.0, xh_ref[...])
    xb = jnp.concatenate([halo, xm_ref[...]], axis=0).astype(BF16)

    def conv_branch(col, slot):
        cs = slice(col, col + FFN_CHUNK)
        h_ref[slot] = jnp.dot(xb, wup_ref[:, cs], preferred_element_type=F32)
        hs = h_ref.at[slot]
        return (cw_ref[2:3, cs] * hs[FFN_HALO:FFN_HALO + tm, :]
                + cw_ref[0:1, cs] * hs[FFN_HALO - 2:FFN_HALO - 2 + tm, :]
                + cw_ref[1:2, cs] * hs[FFN_HALO - 1:FFN_HALO - 1 + tm, :]
                + cb_ref[:, cs])

    for c in range(D_FF // FFN_CHUNK):
        hg = conv_branch(c * FFN_CHUNK, (2 * c) % 4)
        hv = conv_branch(D_FF + c * FFN_CHUNK, (2 * c + 1) % 4)
        act_ref[:, c * FFN_CHUNK:(c + 1) * FFN_CHUNK] = ((hg * _sigmoid(hg)) * hv).astype(BF16)

    ffn = jnp.dot(act_ref[...], wd_ref[...], preferred_element_type=F32)
    x2 = _layer_norm(alpha * xm_ref[...] + ffn, ln2g_ref[...], ln2b_ref[...])
    gate = _sigmoid(jnp.dot(x2.astype(BF16), wpg_ref[...], preferred_element_type=F32))
    ple = gate * jnp.dot(p_ref[...].astype(BF16), wple_ref[...], preferred_element_type=F32)
    o_ref[...] = _layer_norm(alpha * x2 + ple, ln3g_ref[...], ln3b_ref[...])


def _ffn(x1, L, w_up, conv_w, conv_b, w_down, ln2g, ln2b, p2d, w_ple, w_pg, ln3g, ln3b, alpha):
    T = x1.shape[0]
    tm = min(ROW_TM, L)
    hb = tm // FFN_HALO

    def full(shape):
        return pl.BlockSpec(shape, lambda i: (0,) * len(shape), pipeline_mode=pl.Buffered(1))

    return pl.pallas_call(
        functools.partial(_ffn_body, tm=tm, L=L, alpha=alpha),
        grid=(T // tm,),
        in_specs=[pl.BlockSpec((tm, D_MODEL), lambda i: (i, 0)),
                  pl.BlockSpec((FFN_HALO, D_MODEL), lambda i: (jnp.maximum(i * hb - 1, 0), 0)),
                  full(w_up.shape), full(conv_w.shape), full(conv_b.shape), full(w_down.shape),
                  full(ln2g.shape), full(ln2b.shape),
                  pl.BlockSpec((tm, PLE_DIM), lambda i: (i, 0)),
                  full(w_ple.shape), full(w_pg.shape), full(ln3g.shape), full(ln3b.shape)],
        out_specs=pl.BlockSpec((tm, D_MODEL), lambda i: (i, 0)),
        out_shape=jax.ShapeDtypeStruct((T, D_MODEL), F32),
        scratch_shapes=[pltpu.VMEM((tm, D_FF), BF16),
                        pltpu.VMEM((4, tm + FFN_HALO, FFN_CHUNK), F32)],
        compiler_params=pltpu.CompilerParams(
            dimension_semantics=("parallel",),
            vmem_limit_bytes=VMEM_FFN),
        name="ffn",
    )(x1, x1, w_up, conv_w, conv_b, w_down, ln2g, ln2b, p2d, w_ple, w_pg, ln3g, ln3b)


def _permute_w_in(w):
    D = w.shape[0]
    o = 0
    seg = {}
    for name, width in (("gq", 512), ("gk", 512), ("gv", 1024), ("gr", 1024), ("ga", GLA_GATE_RANK),
                        ("dq", 2048), ("ckv", 128), ("iq", 512), ("ik", 64), ("iw", 8),
                        ("gate_a", 1024), ("gate_b", 1024)):
        seg[name] = w[:, o:o + width]
        o += width
    z = lambda n: jnp.zeros((D, n), w.dtype)
    small = jnp.concatenate([seg["ga"], seg["iw"], z(LANES - GLA_GATE_RANK - IDX_HEADS)], axis=1)
    parts = [seg["dq"], seg["gate_a"], seg["gate_b"], seg["gv"], seg["gr"], seg["gq"], seg["gk"], seg["iq"],
             seg["ckv"], jnp.concatenate([seg["ik"], z(LANES - IDX_DIM)], axis=1), small]
    used = sum(p.shape[1] for p in parts)
    parts.append(z(PROJ_COLS - used))
    return jnp.concatenate(parts, axis=1).astype(BF16)


def _pair_block_diag(w_uv):
    H, dc, dh = w_uv.shape
    z = jnp.zeros((dc, dh), w_uv.dtype)
    blocks = [jnp.concatenate([jnp.concatenate([w_uv[2 * j], z], axis=1),
                               jnp.concatenate([z, w_uv[2 * j + 1]], axis=1)], axis=0)
              for j in range(H // 2)]
    return jnp.stack(blocks).astype(BF16)


def kernel(x, p, w_in, w_gla_gate_up, b_gla_gate, g_gla_norm, w_gla_proj, g_ckv_norm, w_uv, w_dsa_proj,
           w_out, ln1_g, ln1_b, w_up, conv_w, conv_b, w_down, ln2_g, ln2_b, w_ple, w_ple_gate, ln3_g,
           ln3_b):
    B, L, D = x.shape
    depth = w_in.shape[0]
    alpha = (2.0 * depth) ** 0.25
    T = B * L
    x2d = x.reshape(T, D)
    row = lambda v: v.reshape(1, -1)
    for i in range(depth):
        w_pad = _permute_w_in(w_in[i])
        proj = _proj(x2d, w_pad)
        wup_pad = jnp.concatenate(
            [w_gla_gate_up[i], jnp.zeros((LANES - GLA_GATE_RANK, GLA_HEADS * GLA_DK), F32)], axis=0).astype(BF16)
        ya = _gla(proj, B, L, wup_pad, row(b_gla_gate[i]), row(g_gla_norm[i]), w_gla_proj[i].astype(BF16))
        ob = _dsa(proj, B, L, row(g_ckv_norm[i]))
        x1 = _merge(ya, ob, proj, x2d, _pair_block_diag(w_uv[i]), w_dsa_proj[i].astype(BF16),
                    w_out[i].astype(BF16), row(ln1_g[i]), row(ln1_b[i]), alpha)
        x2d = _ffn(x1, L, w_up[i].astype(BF16), conv_w[i], row(conv_b[i]), w_down[i].astype(BF16),
                   row(ln2_g[i]), row(ln2_b[i]), p[i].reshape(T, PLE_DIM), w_ple[i].astype(BF16),
                   w_ple_gate[i].astype(BF16), row(ln3_g[i]), row(ln3_b[i]), alpha)
    return x2d.reshape(B, L, D)
```

```python
import functools

import jax
import jax.numpy as jnp
from jax import lax
from jax.experimental import pallas as pl
from jax.experimental.pallas import tpu as pltpu

F32 = jnp.float32
BF16 = jnp.bfloat16
I32 = jnp.int32

D_MODEL = 1024
GLA_HEADS = 4
GLA_DK = 128
GLA_DV = 256
GLA_GATE_RANK = 16
GLA_GATE_TEMP = 16.0
GLA_CHUNK = 64
DSA_HEADS = 16
DSA_HEAD_DIM = 64
DSA_LATENT = 128
IDX_HEADS = 8
IDX_DIM = 64
TOPK_MAX = 256
D_FF = 2816
CONV_W = 3
LN_EPS = 1e-5
PLE_DIM = 256

LANES = 128
SUBLANES = 8
INT_MIN = -2147483648
MASK_LOW31 = 0x7FFFFFFF
MASKED_LOGIT = -1e30
MAX_SAFE_SHIFT = 40.0
SHIFT_SLACK = 1.02
LOG2_E = 1.4426950408889634

PROJ_COLS = 8192
COL_DQ = 0
COL_GATE_A = 2048
COL_GATE_B = 3072
COL_GV = 4096
COL_GR = 5120
COL_GQ = 6144
COL_GK = 6656
COL_IQ = 7168
COL_CKV = 7680
COL_IK = 7808
COL_SMALL = 7936
SMALL_IW = GLA_GATE_RANK

MIB = 1024 * 1024
PROJ_TM, PROJ_TN = 2048, 1024
GLA_TT = 256
DSA_TQ, DSA_TK = 128, 512
ROW_TM = 512
VMEM_SMALL = 40 * MIB
VMEM_FFN = 52 * MIB
VMEM_DSA = 56 * MIB

NT_DIMS = (((1,), (1,)), ((), ()))
TN_DIMS = (((0,), (0,)), ((), ()))


def _layer_norm(v, g, b):
    mu = jnp.mean(v, axis=-1, keepdims=True)
    d = v - mu
    var = jnp.mean(d * d, axis=-1, keepdims=True)
    return d * lax.rsqrt(var + LN_EPS) * g + b


def _sigmoid(v):
    return 1.0 / (1.0 + jnp.exp(-v))


def _proj_body(x_ref, w_ref, o_ref):
    o_ref[...] = jnp.dot(x_ref[...].astype(BF16), w_ref[...], preferred_element_type=F32)


def _proj(x2d, w_pad):
    T, D = x2d.shape
    N = w_pad.shape[1]
    tm = min(PROJ_TM, T)
    tn = PROJ_TN
    return pl.pallas_call(
        _proj_body,
        grid=(T // tm, N // tn),
        in_specs=[pl.BlockSpec((tm, D), lambda i, j: (i, 0)),
                  pl.BlockSpec((D, tn), lambda i, j: (0, j))],
        out_specs=pl.BlockSpec((tm, tn), lambda i, j: (i, j)),
        out_shape=jax.ShapeDtypeStruct((T, N), F32),
        compiler_params=pltpu.CompilerParams(
            dimension_semantics=("parallel", "arbitrary"),
            vmem_limit_bytes=VMEM_FFN),
        name="proj",
    )(x2d, w_pad)


def _gla_body(gq_ref, gk_ref, gv_ref, gr_ref, small_ref, gate_ref, wup_ref, bup_ref, gnorm_ref,
              wproj_ref, out_ref, st_ref, oa_ref, *, tt):
    @pl.when(pl.program_id(1) == 0)
    def _():
        st_ref[...] = jnp.zeros(st_ref.shape, F32)

    C = GLA_CHUNK
    nc = tt // C
    row = lax.broadcasted_iota(I32, (tt, tt), 0)
    col = lax.broadcasted_iota(I32, (tt, tt), 1)
    tril = jnp.logical_and(row >= col, row // C == col // C)
    tril_f = tril.astype(F32)
    z = jnp.dot(small_ref[...].astype(BF16), wup_ref[...], preferred_element_type=F32) + bup_ref[...]
    log_a = (jnp.minimum(z, 0.0) - jnp.log(1.0 + jnp.exp(-jnp.abs(z)))) * (1.0 / GLA_GATE_TEMP)
    bc = jnp.dot(tril_f, log_a, preferred_element_type=F32, precision=lax.Precision.HIGHEST)
    bl = jnp.concatenate(
        [jnp.broadcast_to(bc[(c + 1) * C - 1:(c + 1) * C, :], (C, bc.shape[1])) for c in range(nc)], axis=0)
    e_b = jnp.exp(bc)
    e_nb = jnp.exp(-bc)
    e_bl = jnp.exp(bl - bc)
    for h in range(GLA_HEADS):
        ks = slice(h * GLA_DK, (h + 1) * GLA_DK)
        vs = slice(h * GLA_DV, (h + 1) * GLA_DV)
        q_in = (gq_ref[:, ks] * (GLA_DK ** -0.5) * e_b[:, ks]).astype(BF16)
        k = gk_ref[:, ks]
        k_in = (k * e_nb[:, ks]).astype(BF16)
        k_st = (k * e_bl[:, ks]).astype(BF16)
        v = gv_ref[:, vs].astype(BF16)
        att = lax.dot_general(q_in, k_in, NT_DIMS, preferred_element_type=F32)
        att = jnp.where(tril, att, 0.0).astype(BF16)
        o_intra = jnp.dot(att, v, preferred_element_type=F32)
        st = st_ref[h]
        o_inter = []
        for c in range(nc):
            r = slice(c * C, (c + 1) * C)
            o_inter.append(lax.dot_general(q_in[r], st.astype(BF16), NT_DIMS, preferred_element_type=F32))
            dec = jnp.exp(bc[(c + 1) * C - 1:(c + 1) * C, ks])
            st = st * dec + lax.dot_general(v[r], k_st[r], TN_DIMS, preferred_element_type=F32)
        st_ref[h] = st
        o = o_intra + jnp.concatenate(o_inter, axis=0)
        ms = jnp.mean(o * o, axis=-1, keepdims=True)
        on = o * lax.rsqrt(ms + LN_EPS) * gnorm_ref[:, vs]
        g = gr_ref[:, vs]
        oa_ref[:, vs] = (on * (g * _sigmoid(g))).astype(BF16)
    y = jnp.dot(oa_ref[...], wproj_ref[...], preferred_element_type=F32)
    out_ref[...] = _sigmoid(gate_ref[...]) * y


def _gla(proj, B, L, wup_pad, bup, gnorm, wproj):
    T = B * L
    tt = GLA_TT
    nt = L // tt

    def blk(width, col):
        cb = col // width
        return pl.BlockSpec((tt, width), lambda b, i: (b * nt + i, cb))

    def full(shape):
        return pl.BlockSpec(shape, lambda b, i: (0,) * len(shape))

    return pl.pallas_call(
        functools.partial(_gla_body, tt=tt),
        grid=(B, nt),
        in_specs=[blk(512, COL_GQ), blk(512, COL_GK), blk(1024, COL_GV), blk(1024, COL_GR),
                  blk(128, COL_SMALL), blk(1024, COL_GATE_A),
                  full(wup_pad.shape), full(bup.shape), full(gnorm.shape), full(wproj.shape)],
        out_specs=pl.BlockSpec((tt, D_MODEL), lambda b, i: (b * nt + i, 0)),
        out_shape=jax.ShapeDtypeStruct((T, D_MODEL), F32),
        scratch_shapes=[pltpu.VMEM((GLA_HEADS, GLA_DV, GLA_DK), F32),
                        pltpu.VMEM((tt, GLA_HEADS * GLA_DV), BF16)],
        compiler_params=pltpu.CompilerParams(
            dimension_semantics=("parallel", "arbitrary"),
            vmem_limit_bytes=VMEM_SMALL),
        name="gla",
    )(proj, proj, proj, proj, proj, proj, wup_pad, bup, gnorm, wproj)


SEARCH_BITS = 32
VISITS_PER_TILE = 24
ATT_CHUNKS = 8


def _dsa_body(dq_ref, iq_ref, ckv_ref, ik_ref, small_ref, gck_ref,
              out_ref, kv_ref, ikk_ref, keys_ref, q2_ref, iq2t_ref, sel_ref, acc_ref, m_ref,
              *, L, tq, tk, topk):
    i = pl.program_id(1)
    nq = L // tq
    nt = L // tk
    dl = DSA_LATENT
    has_b = i < nq
    slot_b = i % 2
    slot_a = 1 - slot_b
    qb = jnp.minimum(i, nq - 1)
    q0 = pl.multiple_of(qb * tq, tq)
    n_b = q0 // tk + 1
    n_a = jnp.where(i >= 1, ((i - 1) * tq) // tk + 1, 0)

    kidx0 = lax.broadcasted_iota(I32, (tk, tq), 0)
    qpos = q0 + lax.broadcasted_iota(I32, (tk, tq), 1)

    def tile_b(v):
        return jnp.broadcast_to(v, (tk, tq))

    @pl.when(i == 0)
    def _():
        kv_ref[...] = jnp.zeros(kv_ref.shape, BF16)
        ikk_ref[...] = jnp.zeros(ikk_ref.shape, BF16)
        keys_ref[...] = jnp.full(keys_ref.shape, INT_MIN, I32)
        sel_ref[...] = jnp.zeros(sel_ref.shape, F32)
        onehot = (lax.broadcasted_iota(I32, (tq, tq), 0) == lax.broadcasted_iota(I32, (tq, tq), 1))
        for s in range(2):
            for h in range(DSA_HEADS):
                q2_ref[s, h * tq:(h + 1) * tq, dl:2 * dl] = jnp.where(onehot, 1.0, 0.0).astype(BF16)

    @pl.when(has_b)
    def _():
        c = ckv_ref[...]
        cn = c * lax.rsqrt(jnp.mean(c * c, axis=-1, keepdims=True) + LN_EPS) * gck_ref[...]
        kv_ref[pl.ds(q0, tq), 0:dl] = cn.astype(BF16)
        kv_ref[pl.ds(q0, tq), dl:2 * dl] = jnp.ones((tq, dl), BF16)
        ikk_ref[pl.ds(q0, tq), :] = ik_ref[...].astype(BF16)

        qn2 = jnp.zeros((tq, 1), F32)
        for h in range(DSA_HEADS):
            qh = dq_ref[:, h * dl:(h + 1) * dl] * (dl ** -0.5)
            q2_ref[slot_b, h * tq:(h + 1) * tq, 0:dl] = (qh * LOG2_E).astype(BF16)
            qn2 = jnp.maximum(qn2, jnp.sum(qh * qh, axis=1, keepdims=True))
        k_bound = (dl ** 0.5) * SHIFT_SLACK * jnp.max(jnp.abs(gck_ref[...]))
        qn2_row = jnp.broadcast_to(qn2, (tq, LANES)).T[0:1, :]
        sel_ref[slot_b, 2:3, :] = jnp.sqrt(qn2_row) * k_bound

        for h in range(IDX_HEADS):
            iqh = jnp.concatenate(
                [iq_ref[:, h * IDX_DIM:(h + 1) * IDX_DIM], jnp.zeros((tq, LANES - IDX_DIM), F32)], axis=1)
            iq2t_ref[:, h * tq:(h + 1) * tq] = iqh.T.astype(BF16)
        sm_t = small_ref[...].T
        iw_scale = (IDX_HEADS ** -0.5) * (IDX_DIM ** -0.5)
        w_rows = [sm_t[SMALL_IW + h:SMALL_IW + h + 1, :] * iw_scale for h in range(IDX_HEADS)]

        def score_keys(kt):
            k0 = pl.multiple_of(kt * tk, tk)
            s_all = jnp.dot(ikk_ref[pl.ds(k0, tk), :], iq2t_ref[...], preferred_element_type=F32)
            sc = jnp.zeros((tk, tq), F32)
            for h in range(IDX_HEADS):
                sc = sc + w_rows[h] * jnp.maximum(s_all[:, h * tq:(h + 1) * tq], 0.0)
            bits = lax.bitcast_convert_type(sc, I32)
            key = jnp.where(bits < 0, bits ^ MASK_LOW31, bits)
            keys_ref[slot_b * nt + kt] = jnp.where(k0 + kidx0 <= qpos, key, INT_MIN)

        def score_pair(j, carry):
            score_keys(2 * j)
            score_keys(2 * j + 1)
            return carry

        lax.fori_loop(0, (n_b + 1) // 2, score_pair, 0)

    n_acc = 4
    acc_rows = n_acc * SUBLANES

    def b_keys(kt):
        return keys_ref[slot_b * nt + kt]

    def partial_count(hit):
        return jnp.sum(hit.reshape(tk // acc_rows, acc_rows, tq), axis=0)

    def visit(state):
        thr, cnt_acc, open_q, p, kt = state
        cand = thr + lax.shift_left(jnp.int32(1), jnp.maximum(SEARCH_BITS - 1 - p, 0))
        cnt_acc = cnt_acc + partial_count(jnp.where(b_keys(kt) >= tile_b(cand), 1.0, 0.0))
        last = kt == n_b - 1
        cnt = jnp.sum(cnt_acc, axis=0, keepdims=True)
        take = jnp.logical_and(jnp.logical_and(last, p < SEARCH_BITS), cnt >= topk)
        thr = jnp.where(take, cand, thr)
        open_q = jnp.where(jnp.logical_and(take, cnt == topk), 0.0, open_q)
        cnt_acc = jnp.where(last, 0.0, cnt_acc)
        return thr, cnt_acc, open_q, jnp.where(last, p + 1, p), jnp.where(last, 0, kt + 1)

    open0 = jnp.where(qpos[0:1, :] + 1 >= topk, 1.0, 0.0)
    search0 = (jnp.full((1, tq), INT_MIN, I32), jnp.zeros((acc_rows, tq), F32), open0,
               jnp.int32(0), jnp.int32(0))

    thr_a = tile_b(lax.bitcast_convert_type(sel_ref[slot_a, 0:1, :], I32))
    cut_a = tile_b(sel_ref[slot_a, 1:2, :].astype(I32))
    shift_a = sel_ref[slot_a, 2:3, :]
    acc_ref[...] = jnp.zeros(acc_ref.shape, F32)

    def selected_a(kt):
        key = keys_ref[slot_a * nt + kt]
        return jnp.logical_or(key > thr_a, jnp.logical_and(key == thr_a, kt * tk + kidx0 <= cut_a))

    def att_bounded(state):
        neg_shift = tile_b(-shift_a * LOG2_E)

        def tile(kt, st):
            k0 = pl.multiple_of(kt * tk, tk)
            bias_t = jnp.where(selected_a(kt), neg_shift, MASKED_LOGIT).astype(BF16)
            rhs = jnp.concatenate([kv_ref[pl.ds(k0, tk), 0:dl], bias_t], axis=1)
            kvt = kv_ref[pl.ds(k0, tk), :]
            rows = DSA_HEADS * tq // ATT_CHUNKS
            for c in range(ATT_CHUNKS):
                rs = slice(c * rows, (c + 1) * rows)
                logits = lax.dot_general(q2_ref[slot_a, rs, :], rhs, NT_DIMS, preferred_element_type=F32)
                p = jnp.exp2(logits).astype(BF16)
                acc_ref[rs, :] += jnp.dot(p, kvt, preferred_element_type=F32)
                for _ in range(VISITS_PER_TILE // ATT_CHUNKS):
                    st = visit(st)
            return st

        def tile_pair(j, st):
            return tile(2 * j + 1, tile(2 * j, st))

        state2 = lax.fori_loop(0, n_a // 2, tile_pair, state)
        return lax.cond(n_a % 2 == 1, lambda st: tile(n_a - 1, st), lambda st: st, state2)

    def att_online(state):
        m_ref[...] = jnp.full(m_ref.shape, MASKED_LOGIT, F32)

        def tile(kt, carry):
            k0 = pl.multiple_of(kt * tk, tk)
            kvt = kv_ref[pl.ds(k0, tk), :]
            bias = jnp.where(selected_a(kt), 0.0, -jnp.inf).T
            logits = lax.dot_general(q2_ref[slot_a, :, 0:dl], kvt[:, 0:dl], NT_DIMS,
                                     preferred_element_type=F32) * (1.0 / LOG2_E)
            logits = (logits.reshape(DSA_HEADS, tq, tk) + bias[None]).reshape(DSA_HEADS * tq, tk)
            m_old = m_ref[...]
            m_new = jnp.maximum(m_old, jnp.max(logits, axis=1, keepdims=True))
            alpha = jnp.exp(m_old - m_new)
            p = jnp.exp(logits - m_new).astype(BF16)
            acc_ref[...] = acc_ref[...] * alpha + jnp.dot(p, kvt, preferred_element_type=F32)
            m_ref[...] = m_new
            return carry

        lax.fori_loop(0, n_a, tile, 0)
        return state

    state = lax.cond(jnp.max(shift_a) <= MAX_SAFE_SHIFT, att_bounded, att_online, search0)

    def in_pass(st):
        return jnp.logical_and(jnp.logical_and(st[4] > 0, st[3] < SEARCH_BITS), has_b)

    state = lax.while_loop(in_pass, visit, state)

    def passes_left(st):
        return jnp.logical_and(jnp.logical_and(st[3] < SEARCH_BITS, has_b), jnp.max(st[2]) > 0.0)

    def one_pass(st):
        return lax.fori_loop(0, n_b, lambda _, s: visit(s), st)

    thr = lax.while_loop(passes_left, one_pass, state)[0]
    thr_t = tile_b(thr)

    def count(pred):
        def body(kt, acc):
            return acc + partial_count(jnp.where(pred(b_keys(kt), kt * tk + kidx0), 1.0, 0.0))
        acc = lax.fori_loop(0, jnp.where(has_b, n_b, 0), body, jnp.zeros((acc_rows, tq), F32))
        return jnp.sum(acc, axis=0, keepdims=True)

    n_gt = count(lambda k, s: k > thr_t)
    n_ge = count(lambda k, s: k >= thr_t)
    need = topk - n_gt
    is_min = thr == INT_MIN
    partial = jnp.logical_and(n_ge - n_gt > need, jnp.logical_not(is_min))
    any_partial = jnp.max(jnp.where(partial, 1.0, 0.0)) > 0.0
    nbits = max(1, (L - 1).bit_length())

    def tie_search():
        def idx_pass(p, d):
            cand = d + lax.shift_left(jnp.int32(1), nbits - 1 - p)
            cb = tile_b(cand)
            cnt = count(lambda k, s: jnp.logical_and(k == thr_t, s < cb))
            return jnp.where(cnt < need, cand, d)
        return lax.fori_loop(0, nbits, idx_pass, jnp.zeros((1, tq), I32))

    cut = lax.cond(any_partial, tie_search, lambda: jnp.zeros((1, tq), I32))
    cut = jnp.where(partial, cut, L)
    cut = jnp.where(is_min, -1, cut)
    sel_ref[slot_b, 0:1, :] = lax.bitcast_convert_type(thr, F32)
    sel_ref[slot_b, 1:2, :] = cut.astype(F32)

    @pl.when(i >= 1)
    def _():
        acc = acc_ref[...]
        o = acc[:, 0:dl] * (1.0 / acc[:, dl:2 * dl])
        for h in range(DSA_HEADS):
            out_ref[:, h * dl:(h + 1) * dl] = o[h * tq:(h + 1) * tq, :].astype(BF16)


def _dsa(proj, B, L, gck):
    T = B * L
    tq = DSA_TQ
    tk = min(DSA_TK, L)
    nq = L // tq
    topk = min(TOPK_MAX, L // 4)
    assert (L // tk) % 2 == 0 and tk >= topk and tq == LANES, "score tiles are visited in pairs"

    def blk(width, col):
        cb = col // width
        return pl.BlockSpec((tq, width), lambda b, i: (b * nq + jnp.minimum(i, nq - 1), cb))

    def full(shape):
        return pl.BlockSpec(shape, lambda b, i: (0,) * len(shape))

    return pl.pallas_call(
        functools.partial(_dsa_body, L=L, tq=tq, tk=tk, topk=topk),
        grid=(B, nq + 1),
        in_specs=[blk(2048, COL_DQ), blk(512, COL_IQ), blk(128, COL_CKV), blk(128, COL_IK),
                  blk(128, COL_SMALL), full(gck.shape)],
        out_specs=pl.BlockSpec((tq, DSA_HEADS * DSA_LATENT), lambda b, i: (b * nq + jnp.maximum(i - 1, 0), 0)),
        out_shape=jax.ShapeDtypeStruct((T, DSA_HEADS * DSA_LATENT), BF16),
        scratch_shapes=[pltpu.VMEM((L, 2 * DSA_LATENT), BF16),
                        pltpu.VMEM((L, LANES), BF16),
                        pltpu.VMEM((2 * (L // tk), tk, tq), I32),
                        pltpu.VMEM((2, DSA_HEADS * tq, 2 * DSA_LATENT), BF16),
                        pltpu.VMEM((LANES, IDX_HEADS * tq), BF16),
                        pltpu.VMEM((2, SUBLANES, tq), F32),
                        pltpu.VMEM((DSA_HEADS * tq, 2 * DSA_LATENT), F32),
                        pltpu.VMEM((DSA_HEADS * tq, 1), F32)],
        compiler_params=pltpu.CompilerParams(
            dimension_semantics=("parallel", "arbitrary"),
            vmem_limit_bytes=VMEM_DSA),
        name="dsa",
    )(proj, proj, proj, proj, proj, gck)


def _merge_body(a_ref, ob_ref, gate_ref, x_ref, wuv_ref, wdsa_ref, w_ref, g_ref, be_ref, o_ref, *, alpha):
    ob = ob_ref[...]
    pw = 2 * DSA_LATENT
    u = jnp.concatenate(
        [jnp.dot(ob[:, j * pw:(j + 1) * pw], wuv_ref[j], preferred_element_type=F32).astype(BF16)
         for j in range(DSA_HEADS // 2)], axis=1)
    yb = jnp.dot(u, wdsa_ref[...], preferred_element_type=F32)
    s = (a_ref[...] + _sigmoid(gate_ref[...]) * yb).astype(BF16)
    mixed = jnp.dot(s, w_ref[...], preferred_element_type=F32)
    o_ref[...] = _layer_norm(alpha * x_ref[...] + mixed, g_ref[...], be_ref[...])


def _merge(ya, ob, proj, x2d, wuv_pairs, wdsa, w_out, g, b, alpha):
    T = x2d.shape[0]
    tm = min(ROW_TM, T)
    row = pl.BlockSpec((tm, D_MODEL), lambda i: (i, 0))
    gate_cb = COL_GATE_B // D_MODEL

    def full(shape):
        return pl.BlockSpec(shape, lambda i: (0,) * len(shape))

    return pl.pallas_call(
        functools.partial(_merge_body, alpha=alpha),
        grid=(T // tm,),
        in_specs=[row, pl.BlockSpec((tm, DSA_HEADS * DSA_LATENT), lambda i: (i, 0)),
                  pl.BlockSpec((tm, D_MODEL), lambda i: (i, gate_cb)), row,
                  full(wuv_pairs.shape), full(wdsa.shape), full(w_out.shape), full(g.shape), full(b.shape)],
        out_specs=row,
        out_shape=jax.ShapeDtypeStruct((T, D_MODEL), F32),
        compiler_params=pltpu.CompilerParams(
            dimension_semantics=("parallel",), vmem_limit_bytes=VMEM_SMALL),
        name="merge",
    )(ya, ob, proj, x2d, wuv_pairs, wdsa, w_out, g, b)


FFN_HALO = 16
FFN_CHUNK = 256


def _ffn_body(xm_ref, xh_ref, wup_ref, cw_ref, cb_ref, wd_ref, ln2g_ref, ln2b_ref, p_ref, wple_ref,
              wpg_ref, ln3g_ref, ln3b_ref, o_ref, act_ref, h_ref, *, tm, L, alpha):
    i = pl.program_id(0)
    at_start = (i * tm) % L == 0
    halo = jnp.where(at_start, 0.0, xh_ref[...])
    xb = jnp.concatenate([halo, xm_ref[...]], axis=0).astype(BF16)

    def conv_branch(col, slot):
        cs = slice(col, col + FFN_CHUNK)
        h_ref[slot] = jnp.dot(xb, wup_ref[:, cs], preferred_element_type=F32)
        hs = h_ref.at[slot]
        return (cw_ref[2:3, cs] * hs[FFN_HALO:FFN_HALO + tm, :]
                + cw_ref[0:1, cs] * hs[FFN_HALO - 2:FFN_HALO - 2 + tm, :]
                + cw_ref[1:2, cs] * hs[FFN_HALO - 1:FFN_HALO - 1 + tm, :]
                + cb_ref[:, cs])

    for c in range(D_FF // FFN_CHUNK):
        hg = conv_branch(c * FFN_CHUNK, (2 * c) % 4)
        hv = conv_branch(D_FF + c * FFN_CHUNK, (2 * c + 1) % 4)
        act_ref[:, c * FFN_CHUNK:(c + 1) * FFN_CHUNK] = ((hg * _sigmoid(hg)) * hv).astype(BF16)

    ffn = jnp.dot(act_ref[...], wd_ref[...], preferred_element_type=F32)
    x2 = _layer_norm(alpha * xm_ref[...] + ffn, ln2g_ref[...], ln2b_ref[...])
    gate = _sigmoid(jnp.dot(x2.astype(BF16), wpg_ref[...], preferred_element_type=F32))
    ple = gate * jnp.dot(p_ref[...].astype(BF16), wple_ref[...], preferred_element_type=F32)
    o_ref[...] = _layer_norm(alpha * x2 + ple, ln3g_ref[...], ln3b_ref[...])


def _ffn(x1, L, w_up, conv_w, conv_b, w_down, ln2g, ln2b, p2d, w_ple, w_pg, ln3g, ln3b, alpha):
    T = x1.shape[0]
    tm = min(ROW_TM, L)
    hb = tm // FFN_HALO

    def full(shape):
        return pl.BlockSpec(shape, lambda i: (0,) * len(shape), pipeline_mode=pl.Buffered(1))

    return pl.pallas_call(
        functools.partial(_ffn_body, tm=tm, L=L, alpha=alpha),
        grid=(T // tm,),
        in_specs=[pl.BlockSpec((tm, D_MODEL), lambda i: (i, 0)),
                  pl.BlockSpec((FFN_HALO, D_MODEL), lambda i: (jnp.maximum(i * hb - 1, 0), 0)),
                  full(w_up.shape), full(conv_w.shape), full(conv_b.shape), full(w_down.shape),
                  full(ln2g.shape), full(ln2b.shape),
                  pl.BlockSpec((tm, PLE_DIM), lambda i: (i, 0)),
                  full(w_ple.shape), full(w_pg.shape), full(ln3g.shape), full(ln3b.shape)],
        out_specs=pl.BlockSpec((tm, D_MODEL), lambda i: (i, 0)),
        out_shape=jax.ShapeDtypeStruct((T, D_MODEL), F32),
        scratch_shapes=[pltpu.VMEM((tm, D_FF), BF16),
                        pltpu.VMEM((4, tm + FFN_HALO, FFN_CHUNK), F32)],
        compiler_params=pltpu.CompilerParams(
            dimension_semantics=("parallel",),
            vmem_limit_bytes=VMEM_FFN),
        name="ffn",
    )(x1, x1, w_up, conv_w, conv_b, w_down, ln2g, ln2b, p2d, w_ple, w_pg, ln3g, ln3b)


def _permute_w_in(w):
    D = w.shape[0]
    o = 0
    seg = {}
    for name, width in (("gq", 512), ("gk", 512), ("gv", 1024), ("gr", 1024), ("ga", GLA_GATE_RANK),
                        ("dq", 2048), ("ckv", 128), ("iq", 512), ("ik", 64), ("iw", 8),
                        ("gate_a", 1024), ("gate_b", 1024)):
        seg[name] = w[:, o:o + width]
        o += width
    z = lambda n: jnp.zeros((D, n), w.dtype)
    small = jnp.concatenate([seg["ga"], seg["iw"], z(LANES - GLA_GATE_RANK - IDX_HEADS)], axis=1)
    parts = [seg["dq"], seg["gate_a"], seg["gate_b"], seg["gv"], seg["gr"], seg["gq"], seg["gk"], seg["iq"],
             seg["ckv"], jnp.concatenate([seg["ik"], z(LANES - IDX_DIM)], axis=1), small]
    used = sum(p.shape[1] for p in parts)
    parts.append(z(PROJ_COLS - used))
    return jnp.concatenate(parts, axis=1).astype(BF16)


def _pair_block_diag(w_uv):
    H, dc, dh = w_uv.shape
    z = jnp.zeros((dc, dh), w_uv.dtype)
    blocks = [jnp.concatenate([jnp.concatenate([w_uv[2 * j], z], axis=1),
                               jnp.concatenate([z, w_uv[2 * j + 1]], axis=1)], axis=0)
              for j in range(H // 2)]
    return jnp.stack(blocks).astype(BF16)


def kernel(x, p, w_in, w_gla_gate_up, b_gla_gate, g_gla_norm, w_gla_proj, g_ckv_norm, w_uv, w_dsa_proj,
           w_out, ln1_g, ln1_b, w_up, conv_w, conv_b, w_down, ln2_g, ln2_b, w_ple, w_ple_gate, ln3_g,
           ln3_b):
    B, L, D = x.shape
    depth = w_in.shape[0]
    alpha = (2.0 * depth) ** 0.25
    T = B * L
    x2d = x.reshape(T, D)
    row = lambda v: v.reshape(1, -1)
    for i in range(depth):
        w_pad = _permute_w_in(w_in[i])
        proj = _proj(x2d, w_pad)
        wup_pad = jnp.concatenate(
            [w_gla_gate_up[i], jnp.zeros((LANES - GLA_GATE_RANK, GLA_HEADS * GLA_DK), F32)], axis=0).astype(BF16)
        ya = _gla(proj, B, L, wup_pad, row(b_gla_gate[i]), row(g_gla_norm[i]), w_gla_proj[i].astype(BF16))
        ob = _dsa(proj, B, L, row(g_ckv_norm[i]))
        x1 = _merge(ya, ob, proj, x2d, _pair_block_diag(w_uv[i]), w_dsa_proj[i].astype(BF16),
                    w_out[i].astype(BF16), row(ln1_g[i]), row(ln1_b[i]), alpha)
        x2d = _ffn(x1, L, w_up[i].astype(BF16), conv_w[i], row(conv_b[i]), w_down[i].astype(BF16),
                   row(ln2_g[i]), row(ln2_b[i]), p[i].reshape(T, PLE_DIM), w_ple[i].astype(BF16),
                   w_ple_gate[i].astype(BF16), row(ln3_g[i]), row(ln3_b[i]), alpha)
    return x2d.reshape(B, L, D)
```

```python
import functools

import jax
import jax.numpy as jnp
from jax import lax
from jax.experimental import pallas as pl
from jax.experimental.pallas import tpu as pltpu

F32 = jnp.float32
BF16 = jnp.bfloat16
I32 = jnp.int32

D_MODEL = 1024
GLA_HEADS = 4
GLA_DK = 128
GLA_DV = 256
GLA_GATE_RANK = 16
GLA_GATE_TEMP = 16.0
GLA_CHUNK = 64
DSA_HEADS = 16
DSA_HEAD_DIM = 64
DSA_LATENT = 128
IDX_HEADS = 8
IDX_DIM = 64
TOPK_MAX = 256
D_FF = 2816
CONV_W = 3
LN_EPS = 1e-5
PLE_DIM = 256

LANES = 128
SUBLANES = 8
INT_MIN = -2147483648
MASK_LOW31 = 0x7FFFFFFF
MASKED_LOGIT = -1e30
MAX_SAFE_SHIFT = 40.0
SHIFT_SLACK = 1.02
LOG2_E = 1.4426950408889634

PROJ_COLS = 8192
COL_DQ = 0
COL_GATE_A = 2048
COL_GATE_B = 3072
COL_GV = 4096
COL_GR = 5120
COL_GQ = 6144
COL_GK = 6656
COL_IQ = 7168
COL_CKV = 7680
COL_IK = 7808
COL_SMALL = 7936
SMALL_IW = GLA_GATE_RANK

MIB = 1024 * 1024
PROJ_TM, PROJ_TN = 2048, 1024
GLA_TT = 256
DSA_TQ, DSA_TK = 128, 512
ROW_TM = 512
VMEM_SMALL = 40 * MIB
VMEM_FFN = 52 * MIB
VMEM_DSA = 56 * MIB

NT_DIMS = (((1,), (1,)), ((), ()))
TN_DIMS = (((0,), (0,)), ((), ()))


def _layer_norm(v, g, b):
    mu = jnp.mean(v, axis=-1, keepdims=True)
    d = v - mu
    var = jnp.mean(d * d, axis=-1, keepdims=True)
    return d * lax.rsqrt(var + LN_EPS) * g + b


def _sigmoid(v):
    return 1.0 / (1.0 + jnp.exp(-v))


def _proj_body(x_ref, w_ref, o_ref):
    o_ref[...] = jnp.dot(x_ref[...].astype(BF16), w_ref[...], preferred_element_type=F32)


def _proj(x2d, w_pad):
    T, D = x2d.shape
    N = w_pad.shape[1]
    tm = min(PROJ_TM, T)
    tn = PROJ_TN
    return pl.pallas_call(
        _proj_body,
        grid=(T // tm, N // tn),
        in_specs=[pl.BlockSpec((tm, D), lambda i, j: (i, 0)),
                  pl.BlockSpec((D, tn), lambda i, j: (0, j))],
        out_specs=pl.BlockSpec((tm, tn), lambda i, j: (i, j)),
        out_shape=jax.ShapeDtypeStruct((T, N), F32),
        compiler_params=pltpu.CompilerParams(
            dimension_semantics=("parallel", "arbitrary"),
            vmem_limit_bytes=VMEM_FFN),
        name="proj",
    )(x2d, w_pad)


def _gla_body(gq_ref, gk_ref, gv_ref, gr_ref, small_ref, gate_ref, wup_ref, bup_ref, gnorm_ref,
              wproj_ref, out_ref, st_ref, oa_ref, *, tt):
    @pl.when(pl.program_id(1) == 0)
    def _():
        st_ref[...] = jnp.zeros(st_ref.shape, F32)

    C = GLA_CHUNK
    nc = tt // C
    row = lax.broadcasted_iota(I32, (tt, tt), 0)
    col = lax.broadcasted_iota(I32, (tt, tt), 1)
    tril = jnp.logical_and(row >= col, row // C == col // C)
    tril_f = tril.astype(F32)
    z = jnp.dot(small_ref[...].astype(BF16), wup_ref[...], preferred_element_type=F32) + bup_ref[...]
    log_a = (jnp.minimum(z, 0.0) - jnp.log(1.0 + jnp.exp(-jnp.abs(z)))) * (1.0 / GLA_GATE_TEMP)
    bc = jnp.dot(tril_f, log_a, preferred_element_type=F32, precision=lax.Precision.HIGHEST)
    bl = jnp.concatenate(
        [jnp.broadcast_to(bc[(c + 1) * C - 1:(c + 1) * C, :], (C, bc.shape[1])) for c in range(nc)], axis=0)
    e_b = jnp.exp(bc)
    e_nb = jnp.exp(-bc)
    e_bl = jnp.exp(bl - bc)
    for h in range(GLA_HEADS):
        ks = slice(h * GLA_DK, (h + 1) * GLA_DK)
        vs = slice(h * GLA_DV, (h + 1) * GLA_DV)
        q_in = (gq_ref[:, ks] * (GLA_DK ** -0.5) * e_b[:, ks]).astype(BF16)
        k = gk_ref[:, ks]
        k_in = (k * e_nb[:, ks]).astype(BF16)
        k_st = (k * e_bl[:, ks]).astype(BF16)
        v = gv_ref[:, vs].astype(BF16)
        att = lax.dot_general(q_in, k_in, NT_DIMS, preferred_element_type=F32)
        att = jnp.where(tril, att, 0.0).astype(BF16)
        o_intra = jnp.dot(att, v, preferred_element_type=F32)
        st = st_ref[h]
        o_inter = []
        for c in range(nc):
            r = slice(c * C, (c + 1) * C)
            o_inter.append(lax.dot_general(q_in[r], st.astype(BF16), NT_DIMS, preferred_element_type=F32))
            dec = jnp.exp(bc[(c + 1) * C - 1:(c + 1) * C, ks])
            st = st * dec + lax.dot_general(v[r], k_st[r], TN_DIMS, preferred_element_type=F32)
        st_ref[h] = st
        o = o_intra + jnp.concatenate(o_inter, axis=0)
        ms = jnp.mean(o * o, axis=-1, keepdims=True)
        on = o * lax.rsqrt(ms + LN_EPS) * gnorm_ref[:, vs]
        g = gr_ref[:, vs]
        oa_ref[:, vs] = (on * (g * _sigmoid(g))).astype(BF16)
    y = jnp.dot(oa_ref[...], wproj_ref[...], preferred_element_type=F32)
    out_ref[...] = _sigmoid(gate_ref[...]) * y


def _gla(proj, B, L, wup_pad, bup, gnorm, wproj):
    T = B * L
    tt = GLA_TT
    nt = L // tt

    def blk(width, col):
        cb = col // width
        return pl.BlockSpec((tt, width), lambda b, i: (b * nt + i, cb))

    def full(shape):
        return pl.BlockSpec(shape, lambda b, i: (0,) * len(shape))

    return pl.pallas_call(
        functools.partial(_gla_body, tt=tt),
        grid=(B, nt),
        in_specs=[blk(512, COL_GQ), blk(512, COL_GK), blk(1024, COL_GV), blk(1024, COL_GR),
                  blk(128, COL_SMALL), blk(1024, COL_GATE_A),
                  full(wup_pad.shape), full(bup.shape), full(gnorm.shape), full(wproj.shape)],
        out_specs=pl.BlockSpec((tt, D_MODEL), lambda b, i: (b * nt + i, 0)),
        out_shape=jax.ShapeDtypeStruct((T, D_MODEL), F32),
        scratch_shapes=[pltpu.VMEM((GLA_HEADS, GLA_DV, GLA_DK), F32),
                        pltpu.VMEM((tt, GLA_HEADS * GLA_DV), BF16)],
        compiler_params=pltpu.CompilerParams(
            dimension_semantics=("parallel", "arbitrary"),
            vmem_limit_bytes=VMEM_SMALL),
        name="gla",
    )(proj, proj, proj, proj, proj, proj, wup_pad, bup, gnorm, wproj)


SEARCH_BITS = 32
VISITS_PER_TILE = 24
ATT_CHUNKS = 8
TILE_UNROLL = 8


def _for_tiles(n, body, state, unroll):
    def group(width):
        def run(first, st):
            for u in range(width):
                st = body(first + u, st)
            return st
        return run

    state = lax.fori_loop(0, n // unroll, lambda j, st: group(unroll)(unroll * j, st), state)
    done = (n // unroll) * unroll
    width = unroll // 2
    while width >= 1:
        take = (n - done) >= width
        state = lax.cond(take, lambda st, w=width, d=done: group(w)(d, st), lambda st: st, state)
        done = done + jnp.where(take, width, 0)
        width //= 2
    return state


def _dsa_body(dq_ref, iq_ref, ckv_ref, ik_ref, small_ref, gck_ref,
              out_ref, kv_ref, ikk_ref, keys_ref, q2_ref, iq2t_ref, sel_ref, acc_ref, m_ref,
              *, L, tq, tk, topk):
    i = pl.program_id(1)
    nq = L // tq
    nt = L // tk
    dl = DSA_LATENT
    has_b = i < nq
    slot_b = i % 2
    slot_a = 1 - slot_b
    qb = jnp.minimum(i, nq - 1)
    q0 = pl.multiple_of(qb * tq, tq)
    n_b = q0 // tk + 1
    n_a = jnp.where(i >= 1, ((i - 1) * tq) // tk + 1, 0)

    kidx0 = lax.broadcasted_iota(I32, (tk, tq), 0)
    qpos = q0 + lax.broadcasted_iota(I32, (tk, tq), 1)

    def tile_b(v):
        return jnp.broadcast_to(v, (tk, tq))

    @pl.when(i == 0)
    def _():
        kv_ref[...] = jnp.zeros(kv_ref.shape, BF16)
        ikk_ref[...] = jnp.zeros(ikk_ref.shape, BF16)
        keys_ref[...] = jnp.full(keys_ref.shape, INT_MIN, I32)
        sel_ref[...] = jnp.zeros(sel_ref.shape, F32)
        onehot = (lax.broadcasted_iota(I32, (tq, tq), 0) == lax.broadcasted_iota(I32, (tq, tq), 1))
        for s in range(2):
            for h in range(DSA_HEADS):
                q2_ref[s, h * tq:(h + 1) * tq, dl:2 * dl] = jnp.where(onehot, 1.0, 0.0).astype(BF16)

    @pl.when(has_b)
    def _():
        c = ckv_ref[...]
        cn = c * lax.rsqrt(jnp.mean(c * c, axis=-1, keepdims=True) + LN_EPS) * gck_ref[...]
        kv_ref[pl.ds(q0, tq), 0:dl] = cn.astype(BF16)
        kv_ref[pl.ds(q0, tq), dl:2 * dl] = jnp.ones((tq, dl), BF16)
        ikk_ref[pl.ds(q0, tq), :] = ik_ref[...].astype(BF16)

        qn2 = jnp.zeros((tq, 1), F32)
        for h in range(DSA_HEADS):
            qh = dq_ref[:, h * dl:(h + 1) * dl] * (dl ** -0.5)
            q2_ref[slot_b, h * tq:(h + 1) * tq, 0:dl] = (qh * LOG2_E).astype(BF16)
            qn2 = jnp.maximum(qn2, jnp.sum(qh * qh, axis=1, keepdims=True))
        k_bound = (dl ** 0.5) * SHIFT_SLACK * jnp.max(jnp.abs(gck_ref[...]))
        qn2_row = jnp.broadcast_to(qn2, (tq, LANES)).T[0:1, :]
        sel_ref[slot_b, 2:3, :] = jnp.sqrt(qn2_row) * k_bound

        for h in range(IDX_HEADS):
            iqh = jnp.concatenate(
                [iq_ref[:, h * IDX_DIM:(h + 1) * IDX_DIM], jnp.zeros((tq, LANES - IDX_DIM), F32)], axis=1)
            iq2t_ref[:, h * tq:(h + 1) * tq] = iqh.T.astype(BF16)
        sm_t = small_ref[...].T
        iw_scale = (IDX_HEADS ** -0.5) * (IDX_DIM ** -0.5)
        w_rows = [sm_t[SMALL_IW + h:SMALL_IW + h + 1, :] * iw_scale for h in range(IDX_HEADS)]

        def score_keys(kt):
            k0 = pl.multiple_of(kt * tk, tk)
            s_all = jnp.dot(ikk_ref[pl.ds(k0, tk), :], iq2t_ref[...], preferred_element_type=F32)
            sc = jnp.zeros((tk, tq), F32)
            for h in range(IDX_HEADS):
                sc = sc + w_rows[h] * jnp.maximum(s_all[:, h * tq:(h + 1) * tq], 0.0)
            bits = lax.bitcast_convert_type(sc, I32)
            key = jnp.where(bits < 0, bits ^ MASK_LOW31, bits)
            keys_ref[slot_b * nt + kt] = jnp.where(k0 + kidx0 <= qpos, key, INT_MIN)

        def score_tile(kt, carry):
            score_keys(kt)
            return carry

        _for_tiles(n_b, score_tile, 0, TILE_UNROLL)

    n_acc = 4
    acc_rows = n_acc * SUBLANES

    def b_keys(kt):
        return keys_ref[slot_b * nt + kt]

    def partial_count(hit):
        return jnp.sum(hit.reshape(tk // acc_rows, acc_rows, tq), axis=0)

    def visit(state):
        thr, cnt_acc, open_q, p, kt = state
        cand = thr + lax.shift_left(jnp.int32(1), jnp.maximum(SEARCH_BITS - 1 - p, 0))
        cnt_acc = cnt_acc + partial_count(jnp.where(b_keys(kt) >= tile_b(cand), 1.0, 0.0))
        last = kt == n_b - 1
        cnt = jnp.sum(cnt_acc, axis=0, keepdims=True)
        take = jnp.logical_and(jnp.logical_and(last, p < SEARCH_BITS), cnt >= topk)
        thr = jnp.where(take, cand, thr)
        open_q = jnp.where(jnp.logical_and(take, cnt == topk), 0.0, open_q)
        cnt_acc = jnp.where(last, 0.0, cnt_acc)
        return thr, cnt_acc, open_q, jnp.where(last, p + 1, p), jnp.where(last, 0, kt + 1)

    open0 = jnp.where(qpos[0:1, :] + 1 >= topk, 1.0, 0.0)
    search0 = (jnp.full((1, tq), INT_MIN, I32), jnp.zeros((acc_rows, tq), F32), open0,
               jnp.int32(0), jnp.int32(0))

    thr_a = tile_b(lax.bitcast_convert_type(sel_ref[slot_a, 0:1, :], I32))
    cut_a = tile_b(sel_ref[slot_a, 1:2, :].astype(I32))
    shift_a = sel_ref[slot_a, 2:3, :]
    acc_ref[...] = jnp.zeros(acc_ref.shape, F32)

    def selected_a(kt):
        key = keys_ref[slot_a * nt + kt]
        return jnp.logical_or(key > thr_a, jnp.logical_and(key == thr_a, kt * tk + kidx0 <= cut_a))

    def att_bounded(state):
        neg_shift = tile_b(-shift_a * LOG2_E)

        def tile(kt, st):
            k0 = pl.multiple_of(kt * tk, tk)
            bias_t = jnp.where(selected_a(kt), neg_shift, MASKED_LOGIT).astype(BF16)
            rhs = jnp.concatenate([kv_ref[pl.ds(k0, tk), 0:dl], bias_t], axis=1)
            kvt = kv_ref[pl.ds(k0, tk), :]
            rows = DSA_HEADS * tq // ATT_CHUNKS
            for c in range(ATT_CHUNKS):
                rs = slice(c * rows, (c + 1) * rows)
                logits = lax.dot_general(q2_ref[slot_a, rs, :], rhs, NT_DIMS, preferred_element_type=F32)
                p = jnp.exp2(logits).astype(BF16)
                acc_ref[rs, :] += jnp.dot(p, kvt, preferred_element_type=F32)
                for _ in range(VISITS_PER_TILE // ATT_CHUNKS):
                    st = visit(st)
            return st

        return _for_tiles(n_a, tile, state, TILE_UNROLL)

    def att_online(state):
        m_ref[...] = jnp.full(m_ref.shape, MASKED_LOGIT, F32)

        def tile(kt, carry):
            k0 = pl.multiple_of(kt * tk, tk)
            kvt = kv_ref[pl.ds(k0, tk), :]
            bias = jnp.where(selected_a(kt), 0.0, -jnp.inf).T
            logits = lax.dot_general(q2_ref[slot_a, :, 0:dl], kvt[:, 0:dl], NT_DIMS,
                                     preferred_element_type=F32) * (1.0 / LOG2_E)
            logits = (logits.reshape(DSA_HEADS, tq, tk) + bias[None]).reshape(DSA_HEADS * tq, tk)
            m_old = m_ref[...]
            m_new = jnp.maximum(m_old, jnp.max(logits, axis=1, keepdims=True))
            alpha = jnp.exp(m_old - m_new)
            p = jnp.exp(logits - m_new).astype(BF16)
            acc_ref[...] = acc_ref[...] * alpha + jnp.dot(p, kvt, preferred_element_type=F32)
            m_ref[...] = m_new
            return carry

        lax.fori_loop(0, n_a, tile, 0)
        return state

    state = lax.cond(jnp.max(shift_a) <= MAX_SAFE_SHIFT, att_bounded, att_online, search0)

    def in_pass(st):
        return jnp.logical_and(jnp.logical_and(st[4] > 0, st[3] < SEARCH_BITS), has_b)

    state = lax.while_loop(in_pass, visit, state)

    def passes_left(st):
        return jnp.logical_and(jnp.logical_and(st[3] < SEARCH_BITS, has_b), jnp.max(st[2]) > 0.0)

    def one_pass(st):
        return lax.fori_loop(0, n_b, lambda _, s: visit(s), st)

    thr = lax.while_loop(passes_left, one_pass, state)[0]
    thr_t = tile_b(thr)

    def count(pred):
        def body(kt, acc):
            return acc + partial_count(jnp.where(pred(b_keys(kt), kt * tk + kidx0), 1.0, 0.0))
        acc = lax.fori_loop(0, jnp.where(has_b, n_b, 0), body, jnp.zeros((acc_rows, tq), F32))
        return jnp.sum(acc, axis=0, keepdims=True)

    n_gt = count(lambda k, s: k > thr_t)
    n_ge = count(lambda k, s: k >= thr_t)
    need = topk - n_gt
    is_min = thr == INT_MIN
    partial = jnp.logical_and(n_ge - n_gt > need, jnp.logical_not(is_min))
    any_partial = jnp.max(jnp.where(partial, 1.0, 0.0)) > 0.0
    nbits = max(1, (L - 1).bit_length())

    def tie_search():
        def idx_pass(p, d):
            cand = d + lax.shift_left(jnp.int32(1), nbits - 1 - p)
            cb = tile_b(cand)
            cnt = count(lambda k, s: jnp.logical_and(k == thr_t, s < cb))
            return jnp.where(cnt < need, cand, d)
        return lax.fori_loop(0, nbits, idx_pass, jnp.zeros((1, tq), I32))

    cut = lax.cond(any_partial, tie_search, lambda: jnp.zeros((1, tq), I32))
    cut = jnp.where(partial, cut, L)
    cut = jnp.where(is_min, -1, cut)
    sel_ref[slot_b, 0:1, :] = lax.bitcast_convert_type(thr, F32)
    sel_ref[slot_b, 1:2, :] = cut.astype(F32)

    @pl.when(i >= 1)
    def _():
        acc = acc_ref[...]
        o = acc[:, 0:dl] * (1.0 / acc[:, dl:2 * dl])
        for h in range(DSA_HEADS):
            out_ref[:, h * dl:(h + 1) * dl] = o[h * tq:(h + 1) * tq, :].astype(BF16)


def _dsa(proj, B, L, gck):
    T = B * L
    tq = DSA_TQ
    tk = min(DSA_TK, L)
    nq = L // tq
    topk = min(TOPK_MAX, L // 4)
    assert L % tk == 0 and tk >= topk and tq == LANES

    def blk(width, col):
        cb = col // width
        return pl.BlockSpec((tq, width), lambda b, i: (b * nq + jnp.minimum(i, nq - 1), cb))

    def full(shape):
        return pl.BlockSpec(shape, lambda b, i: (0,) * len(shape))

    return pl.pallas_call(
        functools.partial(_dsa_body, L=L, tq=tq, tk=tk, topk=topk),
        grid=(B, nq + 1),
        in_specs=[blk(2048, COL_DQ), blk(512, COL_IQ), blk(128, COL_CKV), blk(128, COL_IK),
                  blk(128, COL_SMALL), full(gck.shape)],
        out_specs=pl.BlockSpec((tq, DSA_HEADS * DSA_LATENT), lambda b, i: (b * nq + jnp.maximum(i - 1, 0), 0)),
        out_shape=jax.ShapeDtypeStruct((T, DSA_HEADS * DSA_LATENT), BF16),
        scratch_shapes=[pltpu.VMEM((L, 2 * DSA_LATENT), BF16),
                        pltpu.VMEM((L, LANES), BF16),
                        pltpu.VMEM((2 * (L // tk), tk, tq), I32),
                        pltpu.VMEM((2, DSA_HEADS * tq, 2 * DSA_LATENT), BF16),
                        pltpu.VMEM((LANES, IDX_HEADS * tq), BF16),
                        pltpu.VMEM((2, SUBLANES, tq), F32),
                        pltpu.VMEM((DSA_HEADS * tq, 2 * DSA_LATENT), F32),
                        pltpu.VMEM((DSA_HEADS * tq, 1), F32)],
        compiler_params=pltpu.CompilerParams(
            dimension_semantics=("parallel", "arbitrary"),
            vmem_limit_bytes=VMEM_DSA),
        name="dsa",
    )(proj, proj, proj, proj, proj, gck)


def _merge_body(a_ref, ob_ref, gate_ref, x_ref, wuv_ref, wdsa_ref, w_ref, g_ref, be_ref, o_ref, *, alpha):
    ob = ob_ref[...]
    pw = 2 * DSA_LATENT
    u = jnp.concatenate(
        [jnp.dot(ob[:, j * pw:(j + 1) * pw], wuv_ref[j], preferred_element_type=F32).astype(BF16)
         for j in range(DSA_HEADS // 2)], axis=1)
    yb = jnp.dot(u, wdsa_ref[...], preferred_element_type=F32)
    s = (a_ref[...] + _sigmoid(gate_ref[...]) * yb).astype(BF16)
    mixed = jnp.dot(s, w_ref[...], preferred_element_type=F32)
    o_ref[...] = _layer_norm(alpha * x_ref[...] + mixed, g_ref[...], be_ref[...])


def _merge(ya, ob, proj, x2d, wuv_pairs, wdsa, w_out, g, b, alpha):
    T = x2d.shape[0]
    tm = min(ROW_TM, T)
    row = pl.BlockSpec((tm, D_MODEL), lambda i: (i, 0))
    gate_cb = COL_GATE_B // D_MODEL

    def full(shape):
        return pl.BlockSpec(shape, lambda i: (0,) * len(shape))

    return pl.pallas_call(
        functools.partial(_merge_body, alpha=alpha),
        grid=(T // tm,),
        in_specs=[row, pl.BlockSpec((tm, DSA_HEADS * DSA_LATENT), lambda i: (i, 0)),
                  pl.BlockSpec((tm, D_MODEL), lambda i: (i, gate_cb)), row,
                  full(wuv_pairs.shape), full(wdsa.shape), full(w_out.shape), full(g.shape), full(b.shape)],
        out_specs=row,
        out_shape=jax.ShapeDtypeStruct((T, D_MODEL), F32),
        compiler_params=pltpu.CompilerParams(
            dimension_semantics=("parallel",), vmem_limit_bytes=VMEM_SMALL),
        name="merge",
    )(ya, ob, proj, x2d, wuv_pairs, wdsa, w_out, g, b)


FFN_HALO = 16
FFN_CHUNK = 256


def _ffn_body(xm_ref, xh_ref, wup_ref, cw_ref, cb_ref, wd_ref, ln2g_ref, ln2b_ref, p_ref, wple_ref,
              wpg_ref, ln3g_ref, ln3b_ref, o_ref, act_ref, h_ref, *, tm, L, alpha):
    i = pl.program_id(0)
    at_start = (i * tm) % L == 0
    halo = jnp.where(at_start, 0.0, xh_ref[...])
    xb = jnp.concatenate([halo, xm_ref[...]], axis=0).astype(BF16)

    def conv_branch(col, slot):
        cs = slice(col, col + FFN_CHUNK)
        h_ref[slot] = jnp.dot(xb, wup_ref[:, cs], preferred_element_type=F32)
        hs = h_ref.at[slot]
        return (cw_ref[2:3, cs] * hs[FFN_HALO:FFN_HALO + tm, :]
                + cw_ref[0:1, cs] * hs[FFN_HALO - 2:FFN_HALO - 2 + tm, :]
                + cw_ref[1:2, cs] * hs[FFN_HALO - 1:FFN_HALO - 1 + tm, :]
                + cb_ref[:, cs])

    for c in range(D_FF // FFN_CHUNK):
        hg = conv_branch(c * FFN_CHUNK, (2 * c) % 4)
        hv = conv_branch(D_FF + c * FFN_CHUNK, (2 * c + 1) % 4)
        act_ref[:, c * FFN_CHUNK:(c + 1) * FFN_CHUNK] = ((hg * _sigmoid(hg)) * hv).astype(BF16)

    ffn = jnp.dot(act_ref[...], wd_ref[...], preferred_element_type=F32)
    x2 = _layer_norm(alpha * xm_ref[...] + ffn, ln2g_ref[...], ln2b_ref[...])
    gate = _sigmoid(jnp.dot(x2.astype(BF16), wpg_ref[...], preferred_element_type=F32))
    ple = gate * jnp.dot(p_ref[...].astype(BF16), wple_ref[...], preferred_element_type=F32)
    o_ref[...] = _layer_norm(alpha * x2 + ple, ln3g_ref[...], ln3b_ref[...])


def _ffn(x1, L, w_up, conv_w, conv_b, w_down, ln2g, ln2b, p2d, w_ple, w_pg, ln3g, ln3b, alpha):
    T = x1.shape[0]
    tm = min(ROW_TM, L)
    hb = tm // FFN_HALO

    def full(shape):
        return pl.BlockSpec(shape, lambda i: (0,) * len(shape), pipeline_mode=pl.Buffered(1))

    return pl.pallas_call(
        functools.partial(_ffn_body, tm=tm, L=L, alpha=alpha),
        grid=(T // tm,),
        in_specs=[pl.BlockSpec((tm, D_MODEL), lambda i: (i, 0)),
                  pl.BlockSpec((FFN_HALO, D_MODEL), lambda i: (jnp.maximum(i * hb - 1, 0), 0)),
                  full(w_up.shape), full(conv_w.shape), full(conv_b.shape), full(w_down.shape),
                  full(ln2g.shape), full(ln2b.shape),
                  pl.BlockSpec((tm, PLE_DIM), lambda i: (i, 0)),
                  full(w_ple.shape), full(w_pg.shape), full(ln3g.shape), full(ln3b.shape)],
        out_specs=pl.BlockSpec((tm, D_MODEL), lambda i: (i, 0)),
        out_shape=jax.ShapeDtypeStruct((T, D_MODEL), F32),
        scratch_shapes=[pltpu.VMEM((tm, D_FF), BF16),
                        pltpu.VMEM((4, tm + FFN_HALO, FFN_CHUNK), F32)],
        compiler_params=pltpu.CompilerParams(
            dimension_semantics=("parallel",),
            vmem_limit_bytes=VMEM_FFN),
        name="ffn",
    )(x1, x1, w_up, conv_w, conv_b, w_down, ln2g, ln2b, p2d, w_ple, w_pg, ln3g, ln3b)


def _permute_w_in(w):
    D = w.shape[0]
    o = 0
    seg = {}
    for name, width in (("gq", 512), ("gk", 512), ("gv", 1024), ("gr", 1024), ("ga", GLA_GATE_RANK),
                        ("dq", 2048), ("ckv", 128), ("iq", 512), ("ik", 64), ("iw", 8),
                        ("gate_a", 1024), ("gate_b", 1024)):
        seg[name] = w[:, o:o + width]
        o += width
    z = lambda n: jnp.zeros((D, n), w.dtype)
    small = jnp.concatenate([seg["ga"], seg["iw"], z(LANES - GLA_GATE_RANK - IDX_HEADS)], axis=1)
    parts = [seg["dq"], seg["gate_a"], seg["gate_b"], seg["gv"], seg["gr"], seg["gq"], seg["gk"], seg["iq"],
             seg["ckv"], jnp.concatenate([seg["ik"], z(LANES - IDX_DIM)], axis=1), small]
    used = sum(p.shape[1] for p in parts)
    parts.append(z(PROJ_COLS - used))
    return jnp.concatenate(parts, axis=1).astype(BF16)


def _pair_block_diag(w_uv):
    H, dc, dh = w_uv.shape
    z = jnp.zeros((dc, dh), w_uv.dtype)
    blocks = [jnp.concatenate([jnp.concatenate([w_uv[2 * j], z], axis=1),
                               jnp.concatenate([z, w_uv[2 * j + 1]], axis=1)], axis=0)
              for j in range(H // 2)]
    return jnp.stack(blocks).astype(BF16)


def kernel(x, p, w_in, w_gla_gate_up, b_gla_gate, g_gla_norm, w_gla_proj, g_ckv_norm, w_uv, w_dsa_proj,
           w_out, ln1_g, ln1_b, w_up, conv_w, conv_b, w_down, ln2_g, ln2_b, w_ple, w_ple_gate, ln3_g,
           ln3_b):
    B, L, D = x.shape
    depth = w_in.shape[0]
    alpha = (2.0 * depth) ** 0.25
    T = B * L
    x2d = x.reshape(T, D)
    row = lambda v: v.reshape(1, -1)
    for i in range(depth):
        w_pad = _permute_w_in(w_in[i])
        proj = _proj(x2d, w_pad)
        wup_pad = jnp.concatenate(
            [w_gla_gate_up[i], jnp.zeros((LANES - GLA_GATE_RANK, GLA_HEADS * GLA_DK), F32)], axis=0).astype(BF16)
        ya = _gla(proj, B, L, wup_pad, row(b_gla_gate[i]), row(g_gla_norm[i]), w_gla_proj[i].astype(BF16))
        ob = _dsa(proj, B, L, row(g_ckv_norm[i]))
        x1 = _merge(ya, ob, proj, x2d, _pair_block_diag(w_uv[i]), w_dsa_proj[i].astype(BF16),
                    w_out[i].astype(BF16), row(ln1_g[i]), row(ln1_b[i]), alpha)
        x2d = _ffn(x1, L, w_up[i].astype(BF16), conv_w[i], row(conv_b[i]), w_down[i].astype(BF16),
                   row(ln2_g[i]), row(ln2_b[i]), p[i].reshape(T, PLE_DIM), w_ple[i].astype(BF16),
                   w_ple_gate[i].astype(BF16), row(ln3_g[i]), row(ln3_b[i]), alpha)
    return x2d.reshape(B, L, D)
```

```python
import functools

import jax
import jax.numpy as jnp
from jax import lax
from jax.experimental import pallas as pl
from jax.experimental.pallas import tpu as pltpu

F32 = jnp.float32
BF16 = jnp.bfloat16
I32 = jnp.int32

D_MODEL = 1024
GLA_HEADS = 4
GLA_DK = 128
GLA_DV = 256
GLA_GATE_RANK = 16
GLA_GATE_TEMP = 16.0
GLA_CHUNK = 64
DSA_HEADS = 16
DSA_HEAD_DIM = 64
DSA_LATENT = 128
IDX_HEADS = 8
IDX_DIM = 64
TOPK_MAX = 256
D_FF = 2816
CONV_W = 3
LN_EPS = 1e-5
PLE_DIM = 256

LANES = 128
SUBLANES = 8
INT_MIN = -2147483648
MASK_LOW31 = 0x7FFFFFFF
MASKED_LOGIT = -1e30
MAX_SAFE_SHIFT = 40.0
SHIFT_SLACK = 1.02
LOG2_E = 1.4426950408889634

PROJ_COLS = 8192
COL_DQ = 0
COL_GATE_A = 2048
COL_GATE_B = 3072
COL_GV = 4096
COL_GR = 5120
COL_GQ = 6144
COL_GK = 6656
COL_IQ = 7168
COL_CKV = 7680
COL_IK = 7808
COL_SMALL = 7936
SMALL_IW = GLA_GATE_RANK

MIB = 1024 * 1024
PROJ_TM, PROJ_TN = 2048, 1024
GLA_TT = 256
DSA_TQ, DSA_TK = 128, 512
ROW_TM = 512
VMEM_SMALL = 40 * MIB
VMEM_FFN = 52 * MIB
VMEM_DSA = 56 * MIB

NT_DIMS = (((1,), (1,)), ((), ()))
TN_DIMS = (((0,), (0,)), ((), ()))


def _layer_norm(v, g, b):
    mu = jnp.mean(v, axis=-1, keepdims=True)
    d = v - mu
    var = jnp.mean(d * d, axis=-1, keepdims=True)
    return d * lax.rsqrt(var + LN_EPS) * g + b


def _sigmoid(v):
    return 1.0 / (1.0 + jnp.exp(-v))


def _proj_body(x_ref, w_ref, o_ref):
    o_ref[...] = jnp.dot(x_ref[...].astype(BF16), w_ref[...], preferred_element_type=F32)


def _proj(x2d, w_pad):
    T, D = x2d.shape
    N = w_pad.shape[1]
    tm = min(PROJ_TM, T)
    tn = PROJ_TN
    return pl.pallas_call(
        _proj_body,
        grid=(T // tm, N // tn),
        in_specs=[pl.BlockSpec((tm, D), lambda i, j: (i, 0)),
                  pl.BlockSpec((D, tn), lambda i, j: (0, j))],
        out_specs=pl.BlockSpec((tm, tn), lambda i, j: (i, j)),
        out_shape=jax.ShapeDtypeStruct((T, N), F32),
        compiler_params=pltpu.CompilerParams(
            dimension_semantics=("parallel", "arbitrary"),
            vmem_limit_bytes=VMEM_FFN),
        name="proj",
    )(x2d, w_pad)


def _gla_body(gq_ref, gk_ref, gv_ref, gr_ref, small_ref, gate_ref, wup_ref, bup_ref, gnorm_ref,
              wproj_ref, out_ref, st_ref, oa_ref, *, tt):
    @pl.when(pl.program_id(1) == 0)
    def _():
        st_ref[...] = jnp.zeros(st_ref.shape, F32)

    C = GLA_CHUNK
    nc = tt // C
    row = lax.broadcasted_iota(I32, (tt, tt), 0)
    col = lax.broadcasted_iota(I32, (tt, tt), 1)
    tril = jnp.logical_and(row >= col, row // C == col // C)
    tril_f = tril.astype(F32)
    z = jnp.dot(small_ref[...].astype(BF16), wup_ref[...], preferred_element_type=F32) + bup_ref[...]
    log_a = (jnp.minimum(z, 0.0) - jnp.log(1.0 + jnp.exp(-jnp.abs(z)))) * (1.0 / GLA_GATE_TEMP)
    bc = jnp.dot(tril_f, log_a, preferred_element_type=F32, precision=lax.Precision.HIGHEST)
    bl = jnp.concatenate(
        [jnp.broadcast_to(bc[(c + 1) * C - 1:(c + 1) * C, :], (C, bc.shape[1])) for c in range(nc)], axis=0)
    e_b = jnp.exp(bc)
    e_nb = jnp.exp(-bc)
    e_bl = jnp.exp(bl - bc)
    for h in range(GLA_HEADS):
        ks = slice(h * GLA_DK, (h + 1) * GLA_DK)
        vs = slice(h * GLA_DV, (h + 1) * GLA_DV)
        q_in = (gq_ref[:, ks] * (GLA_DK ** -0.5) * e_b[:, ks]).astype(BF16)
        k = gk_ref[:, ks]
        k_in = (k * e_nb[:, ks]).astype(BF16)
        k_st = (k * e_bl[:, ks]).astype(BF16)
        v = gv_ref[:, vs].astype(BF16)
        att = lax.dot_general(q_in, k_in, NT_DIMS, preferred_element_type=F32)
        att = jnp.where(tril, att, 0.0).astype(BF16)
        o_intra = jnp.dot(att, v, preferred_element_type=F32)
        st = st_ref[h]
        o_inter = []
        for c in range(nc):
            r = slice(c * C, (c + 1) * C)
            o_inter.append(lax.dot_general(q_in[r], st.astype(BF16), NT_DIMS, preferred_element_type=F32))
            dec = jnp.exp(bc[(c + 1) * C - 1:(c + 1) * C, ks])
            st = st * dec + lax.dot_general(v[r], k_st[r], TN_DIMS, preferred_element_type=F32)
        st_ref[h] = st
        o = o_intra + jnp.concatenate(o_inter, axis=0)
        ms = jnp.mean(o * o, axis=-1, keepdims=True)
        on = o * lax.rsqrt(ms + LN_EPS) * gnorm_ref[:, vs]
        g = gr_ref[:, vs]
        oa_ref[:, vs] = (on * (g * _sigmoid(g))).astype(BF16)
    y = jnp.dot(oa_ref[...], wproj_ref[...], preferred_element_type=F32)
    out_ref[...] = _sigmoid(gate_ref[...]) * y


def _gla(proj, B, L, wup_pad, bup, gnorm, wproj):
    T = B * L
    tt = GLA_TT
    nt = L // tt

    def blk(width, col):
        cb = col // width
        return pl.BlockSpec((tt, width), lambda b, i: (b * nt + i, cb))

    def full(shape):
        return pl.BlockSpec(shape, lambda b, i: (0,) * len(shape))

    return pl.pallas_call(
        functools.partial(_gla_body, tt=tt),
        grid=(B, nt),
        in_specs=[blk(512, COL_GQ), blk(512, COL_GK), blk(1024, COL_GV), blk(1024, COL_GR),
                  blk(128, COL_SMALL), blk(1024, COL_GATE_A),
                  full(wup_pad.shape), full(bup.shape), full(gnorm.shape), full(wproj.shape)],
        out_specs=pl.BlockSpec((tt, D_MODEL), lambda b, i: (b * nt + i, 0)),
        out_shape=jax.ShapeDtypeStruct((T, D_MODEL), F32),
        scratch_shapes=[pltpu.VMEM((GLA_HEADS, GLA_DV, GLA_DK), F32),
                        pltpu.VMEM((tt, GLA_HEADS * GLA_DV), BF16)],
        compiler_params=pltpu.CompilerParams(
            dimension_semantics=("parallel", "arbitrary"),
            vmem_limit_bytes=VMEM_SMALL),
        name="gla",
    )(proj, proj, proj, proj, proj, proj, wup_pad, bup, gnorm, wproj)


SEARCH_BITS = 32
VISITS_PER_TILE = 24
ATT_CHUNKS = 8
TILE_UNROLL = 8


def _for_tiles(n, body, state, unroll):
    def group(width):
        def run(first, st):
            for u in range(width):
                st = body(first + u, st)
            return st
        return run

    state = lax.fori_loop(0, n // unroll, lambda j, st: group(unroll)(unroll * j, st), state)
    done = (n // unroll) * unroll
    width = unroll // 2
    while width >= 1:
        take = (n - done) >= width
        state = lax.cond(take, lambda st, w=width, d=done: group(w)(d, st), lambda st: st, state)
        done = done + jnp.where(take, width, 0)
        width //= 2
    return state


def _dsa_body(dq_ref, iq_ref, ckv_ref, ik_ref, small_ref, gck_ref,
              out_ref, kv_ref, ikk_ref, keys_ref, q2_ref, iq2t_ref, sel_ref, acc_ref, m_ref,
              *, L, tq, tk, topk):
    i = pl.program_id(1)
    nq = L // tq
    nt = L // tk
    dl = DSA_LATENT
    has_b = i < nq
    slot_b = i % 2
    slot_a = 1 - slot_b
    qb = jnp.minimum(i, nq - 1)
    q0 = pl.multiple_of(qb * tq, tq)
    n_b = q0 // tk + 1
    n_a = jnp.where(i >= 1, ((i - 1) * tq) // tk + 1, 0)

    kidx0 = lax.broadcasted_iota(I32, (tk, tq), 0)
    qpos = q0 + lax.broadcasted_iota(I32, (tk, tq), 1)

    def tile_b(v):
        return jnp.broadcast_to(v, (tk, tq))

    @pl.when(i == 0)
    def _():
        kv_ref[...] = jnp.zeros(kv_ref.shape, BF16)
        ikk_ref[...] = jnp.zeros(ikk_ref.shape, BF16)
        keys_ref[...] = jnp.full(keys_ref.shape, INT_MIN, I32)
        sel_ref[...] = jnp.zeros(sel_ref.shape, F32)
        onehot = (lax.broadcasted_iota(I32, (tq, tq), 0) == lax.broadcasted_iota(I32, (tq, tq), 1))
        for s in range(2):
            for h in range(DSA_HEADS):
                q2_ref[s, h * tq:(h + 1) * tq, dl:2 * dl] = jnp.where(onehot, 1.0, 0.0).astype(BF16)

    @pl.when(has_b)
    def _():
        c = ckv_ref[...]
        cn = c * lax.rsqrt(jnp.mean(c * c, axis=-1, keepdims=True) + LN_EPS) * gck_ref[...]
        kv_ref[pl.ds(q0, tq), 0:dl] = cn.astype(BF16)
        kv_ref[pl.ds(q0, tq), dl:2 * dl] = jnp.ones((tq, dl), BF16)
        ikk_ref[pl.ds(q0, tq), :] = ik_ref[...].astype(BF16)

        qn2 = jnp.zeros((tq, 1), F32)
        for h in range(DSA_HEADS):
            qh = dq_ref[:, h * dl:(h + 1) * dl] * (dl ** -0.5)
            q2_ref[slot_b, h * tq:(h + 1) * tq, 0:dl] = (qh * LOG2_E).astype(BF16)
            qn2 = jnp.maximum(qn2, jnp.sum(qh * qh, axis=1, keepdims=True))
        k_bound = (dl ** 0.5) * SHIFT_SLACK * jnp.max(jnp.abs(gck_ref[...]))
        qn2_row = jnp.broadcast_to(qn2, (tq, LANES)).T[0:1, :]
        sel_ref[slot_b, 2:3, :] = jnp.sqrt(qn2_row) * k_bound

        for h in range(IDX_HEADS):
            iqh = jnp.concatenate(
                [iq_ref[:, h * IDX_DIM:(h + 1) * IDX_DIM], jnp.zeros((tq, LANES - IDX_DIM), F32)], axis=1)
            iq2t_ref[:, h * tq:(h + 1) * tq] = iqh.T.astype(BF16)
        sm_t = small_ref[...].T
        iw_scale = (IDX_HEADS ** -0.5) * (IDX_DIM ** -0.5)
        w_rows = [sm_t[SMALL_IW + h:SMALL_IW + h + 1, :] * iw_scale for h in range(IDX_HEADS)]

        def score_keys(kt):
            k0 = pl.multiple_of(kt * tk, tk)
            s_all = jnp.dot(ikk_ref[pl.ds(k0, tk), :], iq2t_ref[...], preferred_element_type=F32)
            sc = jnp.zeros((tk, tq), F32)
            for h in range(IDX_HEADS):
                sc = sc + w_rows[h] * jnp.maximum(s_all[:, h * tq:(h + 1) * tq], 0.0)
            bits = lax.bitcast_convert_type(sc, I32)
            key = jnp.where(bits < 0, bits ^ MASK_LOW31, bits)
            keys_ref[slot_b * nt + kt] = jnp.where(k0 + kidx0 <= qpos, key, INT_MIN)

        def score_tile(kt, carry):
            score_keys(kt)
            return carry

        _for_tiles(n_b, score_tile, 0, TILE_UNROLL)

    n_acc = 4
    acc_rows = n_acc * SUBLANES

    def b_keys(kt):
        return keys_ref[slot_b * nt + kt]

    def partial_count(hit):
        return jnp.sum(hit.reshape(tk // acc_rows, acc_rows, tq), axis=0)

    def visit(state):
        thr, cnt_acc, open_q, p, kt = state
        cand = thr + lax.shift_left(jnp.int32(1), jnp.maximum(SEARCH_BITS - 1 - p, 0))
        cnt_acc = cnt_acc + partial_count(jnp.where(b_keys(kt) >= tile_b(cand), 1.0, 0.0))
        last = kt == n_b - 1
        cnt = jnp.sum(cnt_acc, axis=0, keepdims=True)
        take = jnp.logical_and(jnp.logical_and(last, p < SEARCH_BITS), cnt >= topk)
        thr = jnp.where(take, cand, thr)
        open_q = jnp.where(jnp.logical_and(take, cnt == topk), 0.0, open_q)
        cnt_acc = jnp.where(last, 0.0, cnt_acc)
        return thr, cnt_acc, open_q, jnp.where(last, p + 1, p), jnp.where(last, 0, kt + 1)

    open0 = jnp.where(qpos[0:1, :] + 1 >= topk, 1.0, 0.0)
    search0 = (jnp.full((1, tq), INT_MIN, I32), jnp.zeros((acc_rows, tq), F32), open0,
               jnp.int32(0), jnp.int32(0))

    thr_a = tile_b(lax.bitcast_convert_type(sel_ref[slot_a, 0:1, :], I32))
    cut_a = tile_b(sel_ref[slot_a, 1:2, :].astype(I32))
    shift_a = sel_ref[slot_a, 2:3, :]
    acc_ref[...] = jnp.zeros(acc_ref.shape, F32)

    def selected_a(kt):
        key = keys_ref[slot_a * nt + kt]
        return jnp.logical_or(key > thr_a, jnp.logical_and(key == thr_a, kt * tk + kidx0 <= cut_a))

    def att_bounded(state):
        neg_shift = tile_b(-shift_a * LOG2_E)

        def tile(kt, st):
            k0 = pl.multiple_of(kt * tk, tk)
            bias_t = jnp.where(selected_a(kt), neg_shift, MASKED_LOGIT).astype(BF16)
            rhs = jnp.concatenate([kv_ref[pl.ds(k0, tk), 0:dl], bias_t], axis=1)
            kvt = kv_ref[pl.ds(k0, tk), :]
            rows = DSA_HEADS * tq // ATT_CHUNKS
            for c in range(ATT_CHUNKS):
                rs = slice(c * rows, (c + 1) * rows)
                logits = lax.dot_general(q2_ref[slot_a, rs, :], rhs, NT_DIMS, preferred_element_type=F32)
                p = jnp.exp2(logits).astype(BF16)
                acc_ref[rs, :] += jnp.dot(p, kvt, preferred_element_type=F32)
                for _ in range(VISITS_PER_TILE // ATT_CHUNKS):
                    st = visit(st)
            return st

        return _for_tiles(n_a, tile, state, TILE_UNROLL)

    def att_online(state):
        m_ref[...] = jnp.full(m_ref.shape, MASKED_LOGIT, F32)

        def tile(kt, carry):
            k0 = pl.multiple_of(kt * tk, tk)
            kvt = kv_ref[pl.ds(k0, tk), :]
            bias = jnp.where(selected_a(kt), 0.0, -jnp.inf).T
            logits = lax.dot_general(q2_ref[slot_a, :, 0:dl], kvt[:, 0:dl], NT_DIMS,
                                     preferred_element_type=F32) * (1.0 / LOG2_E)
            logits = (logits.reshape(DSA_HEADS, tq, tk) + bias[None]).reshape(DSA_HEADS * tq, tk)
            m_old = m_ref[...]
            m_new = jnp.maximum(m_old, jnp.max(logits, axis=1, keepdims=True))
            alpha = jnp.exp(m_old - m_new)
            p = jnp.exp(logits - m_new).astype(BF16)
            acc_ref[...] = acc_ref[...] * alpha + jnp.dot(p, kvt, preferred_element_type=F32)
            m_ref[...] = m_new
            return carry

        lax.fori_loop(0, n_a, tile, 0)
        return state

    state = lax.cond(jnp.max(shift_a) <= MAX_SAFE_SHIFT, att_bounded, att_online, search0)

    def in_pass(st):
        return jnp.logical_and(jnp.logical_and(st[4] > 0, st[3] < SEARCH_BITS), has_b)

    state = lax.while_loop(in_pass, visit, state)

    def passes_left(st):
        return jnp.logical_and(jnp.logical_and(st[3] < SEARCH_BITS, has_b), jnp.max(st[2]) > 0.0)

    def one_pass(st):
        thr, cnt_acc, open_q, p, kt = st
        cand = thr + lax.shift_left(jnp.int32(1), SEARCH_BITS - 1 - p)
        cand_t = tile_b(cand)
        hits = _for_tiles(
            n_b, lambda t, a: a + partial_count(jnp.where(b_keys(t) >= cand_t, 1.0, 0.0)), cnt_acc,
            TILE_UNROLL)
        cnt = jnp.sum(hits, axis=0, keepdims=True)
        take = cnt >= topk
        thr = jnp.where(take, cand, thr)
        open_q = jnp.where(jnp.logical_and(take, cnt == topk), 0.0, open_q)
        return thr, cnt_acc, open_q, p + 1, kt

    thr = lax.while_loop(passes_left, one_pass, state)[0]
    thr_t = tile_b(thr)

    def count(pred):
        def body(kt, acc):
            return acc + partial_count(jnp.where(pred(b_keys(kt), kt * tk + kidx0), 1.0, 0.0))
        acc = lax.fori_loop(0, jnp.where(has_b, n_b, 0), body, jnp.zeros((acc_rows, tq), F32))
        return jnp.sum(acc, axis=0, keepdims=True)

    n_gt = count(lambda k, s: k > thr_t)
    n_ge = count(lambda k, s: k >= thr_t)
    need = topk - n_gt
    is_min = thr == INT_MIN
    partial = jnp.logical_and(n_ge - n_gt > need, jnp.logical_not(is_min))
    any_partial = jnp.max(jnp.where(partial, 1.0, 0.0)) > 0.0
    nbits = max(1, (L - 1).bit_length())

    def tie_search():
        def idx_pass(p, d):
            cand = d + lax.shift_left(jnp.int32(1), nbits - 1 - p)
            cb = tile_b(cand)
            cnt = count(lambda k, s: jnp.logical_and(k == thr_t, s < cb))
            return jnp.where(cnt < need, cand, d)
        return lax.fori_loop(0, nbits, idx_pass, jnp.zeros((1, tq), I32))

    cut = lax.cond(any_partial, tie_search, lambda: jnp.zeros((1, tq), I32))
    cut = jnp.where(partial, cut, L)
    cut = jnp.where(is_min, -1, cut)
    sel_ref[slot_b, 0:1, :] = lax.bitcast_convert_type(thr, F32)
    sel_ref[slot_b, 1:2, :] = cut.astype(F32)

    @pl.when(i >= 1)
    def _():
        acc = acc_ref[...]
        o = acc[:, 0:dl] * (1.0 / acc[:, dl:2 * dl])
        for h in range(DSA_HEADS):
            out_ref[:, h * dl:(h + 1) * dl] = o[h * tq:(h + 1) * tq, :].astype(BF16)


def _dsa(proj, B, L, gck):
    T = B * L
    tq = DSA_TQ
    tk = min(DSA_TK, L)
    nq = L // tq
    topk = min(TOPK_MAX, L // 4)
    assert L % tk == 0 and tk >= topk and tq == LANES

    def blk(width, col):
        cb = col // width
        return pl.BlockSpec((tq, width), lambda b, i: (b * nq + jnp.minimum(i, nq - 1), cb))

    def full(shape):
        return pl.BlockSpec(shape, lambda b, i: (0,) * len(shape))

    return pl.pallas_call(
        functools.partial(_dsa_body, L=L, tq=tq, tk=tk, topk=topk),
        grid=(B, nq + 1),
        in_specs=[blk(2048, COL_DQ), blk(512, COL_IQ), blk(128, COL_CKV), blk(128, COL_IK),
                  blk(128, COL_SMALL), full(gck.shape)],
        out_specs=pl.BlockSpec((tq, DSA_HEADS * DSA_LATENT), lambda b, i: (b * nq + jnp.maximum(i - 1, 0), 0)),
        out_shape=jax.ShapeDtypeStruct((T, DSA_HEADS * DSA_LATENT), BF16),
        scratch_shapes=[pltpu.VMEM((L, 2 * DSA_LATENT), BF16),
                        pltpu.VMEM((L, LANES), BF16),
                        pltpu.VMEM((2 * (L // tk), tk, tq), I32),
                        pltpu.VMEM((2, DSA_HEADS * tq, 2 * DSA_LATENT), BF16),
                        pltpu.VMEM((LANES, IDX_HEADS * tq), BF16),
                        pltpu.VMEM((2, SUBLANES, tq), F32),
                        pltpu.VMEM((DSA_HEADS * tq, 2 * DSA_LATENT), F32),
                        pltpu.VMEM((DSA_HEADS * tq, 1), F32)],
        compiler_params=pltpu.CompilerParams(
            dimension_semantics=("parallel", "arbitrary"),
            vmem_limit_bytes=VMEM_DSA),
        name="dsa",
    )(proj, proj, proj, proj, proj, gck)


def _merge_body(a_ref, ob_ref, gate_ref, x_ref, wuv_ref, wdsa_ref, w_ref, g_ref, be_ref, o_ref, *, alpha):
    ob = ob_ref[...]
    pw = 2 * DSA_LATENT
    u = jnp.concatenate(
        [jnp.dot(ob[:, j * pw:(j + 1) * pw], wuv_ref[j], preferred_element_type=F32).astype(BF16)
         for j in range(DSA_HEADS // 2)], axis=1)
    yb = jnp.dot(u, wdsa_ref[...], preferred_element_type=F32)
    s = (a_ref[...] + _sigmoid(gate_ref[...]) * yb).astype(BF16)
    mixed = jnp.dot(s, w_ref[...], preferred_element_type=F32)
    o_ref[...] = _layer_norm(alpha * x_ref[...] + mixed, g_ref[...], be_ref[...])


def _merge(ya, ob, proj, x2d, wuv_pairs, wdsa, w_out, g, b, alpha):
    T = x2d.shape[0]
    tm = min(ROW_TM, T)
    row = pl.BlockSpec((tm, D_MODEL), lambda i: (i, 0))
    gate_cb = COL_GATE_B // D_MODEL

    def full(shape):
        return pl.BlockSpec(shape, lambda i: (0,) * len(shape))

    return pl.pallas_call(
        functools.partial(_merge_body, alpha=alpha),
        grid=(T // tm,),
        in_specs=[row, pl.BlockSpec((tm, DSA_HEADS * DSA_LATENT), lambda i: (i, 0)),
                  pl.BlockSpec((tm, D_MODEL), lambda i: (i, gate_cb)), row,
                  full(wuv_pairs.shape), full(wdsa.shape), full(w_out.shape), full(g.shape), full(b.shape)],
        out_specs=row,
        out_shape=jax.ShapeDtypeStruct((T, D_MODEL), F32),
        compiler_params=pltpu.CompilerParams(
            dimension_semantics=("parallel",), vmem_limit_bytes=VMEM_SMALL),
        name="merge",
    )(ya, ob, proj, x2d, wuv_pairs, wdsa, w_out, g, b)


FFN_HALO = 16
FFN_CHUNK = 256


def _ffn_body(xm_ref, xh_ref, wup_ref, cw_ref, cb_ref, wd_ref, ln2g_ref, ln2b_ref, p_ref, wple_ref,
              wpg_ref, ln3g_ref, ln3b_ref, o_ref, act_ref, h_ref, *, tm, L, alpha):
    i = pl.program_id(0)
    at_start = (i * tm) % L == 0
    halo = jnp.where(at_start, 0.0, xh_ref[...])
    xb = jnp.concatenate([halo, xm_ref[...]], axis=0).astype(BF16)

    def conv_branch(col, slot):
        cs = slice(col, col + FFN_CHUNK)
        h_ref[slot] = jnp.dot(xb, wup_ref[:, cs], preferred_element_type=F32)
        hs = h_ref.at[slot]
        return (cw_ref[2:3, cs] * hs[FFN_HALO:FFN_HALO + tm, :]
                + cw_ref[0:1, cs] * hs[FFN_HALO - 2:FFN_HALO - 2 + tm, :]
                + cw_ref[1:2, cs] * hs[FFN_HALO - 1:FFN_HALO - 1 + tm, :]
                + cb_ref[:, cs])

    for c in range(D_FF // FFN_CHUNK):
        hg = conv_branch(c * FFN_CHUNK, (2 * c) % 4)
        hv = conv_branch(D_FF + c * FFN_CHUNK, (2 * c + 1) % 4)
        act_ref[:, c * FFN_CHUNK:(c + 1) * FFN_CHUNK] = ((hg * _sigmoid(hg)) * hv).astype(BF16)

    ffn = jnp.dot(act_ref[...], wd_ref[...], preferred_element_type=F32)
    x2 = _layer_norm(alpha * xm_ref[...] + ffn, ln2g_ref[...], ln2b_ref[...])
    gate = _sigmoid(jnp.dot(x2.astype(BF16), wpg_ref[...], preferred_element_type=F32))
    ple = gate * jnp.dot(p_ref[...].astype(BF16), wple_ref[...], preferred_element_type=F32)
    o_ref[...] = _layer_norm(alpha * x2 + ple, ln3g_ref[...], ln3b_ref[...])


def _ffn(x1, L, w_up, conv_w, conv_b, w_down, ln2g, ln2b, p2d, w_ple, w_pg, ln3g, ln3b, alpha):
    T = x1.shape[0]
    tm = min(ROW_TM, L)
    hb = tm // FFN_HALO

    def full(shape):
        return pl.BlockSpec(shape, lambda i: (0,) * len(shape), pipeline_mode=pl.Buffered(1))

    return pl.pallas_call(
        functools.partial(_ffn_body, tm=tm, L=L, alpha=alpha),
        grid=(T // tm,),
        in_specs=[pl.BlockSpec((tm, D_MODEL), lambda i: (i, 0)),
                  pl.BlockSpec((FFN_HALO, D_MODEL), lambda i: (jnp.maximum(i * hb - 1, 0), 0)),
                  full(w_up.shape), full(conv_w.shape), full(conv_b.shape), full(w_down.shape),
                  full(ln2g.shape), full(ln2b.shape),
                  pl.BlockSpec((tm, PLE_DIM), lambda i: (i, 0)),
                  full(w_ple.shape), full(w_pg.shape), full(ln3g.shape), full(ln3b.shape)],
        out_specs=pl.BlockSpec((tm, D_MODEL), lambda i: (i, 0)),
        out_shape=jax.ShapeDtypeStruct((T, D_MODEL), F32),
        scratch_shapes=[pltpu.VMEM((tm, D_FF), BF16),
                        pltpu.VMEM((4, tm + FFN_HALO, FFN_CHUNK), F32)],
        compiler_params=pltpu.CompilerParams(
            dimension_semantics=("parallel",),
            vmem_limit_bytes=VMEM_FFN),
        name="ffn",
    )(x1, x1, w_up, conv_w, conv_b, w_down, ln2g, ln2b, p2d, w_ple, w_pg, ln3g, ln3b)


def _permute_w_in(w):
    D = w.shape[0]
    o = 0
    seg = {}
    for name, width in (("gq", 512), ("gk", 512), ("gv", 1024), ("gr", 1024), ("ga", GLA_GATE_RANK),
                        ("dq", 2048), ("ckv", 128), ("iq", 512), ("ik", 64), ("iw", 8),
                        ("gate_a", 1024), ("gate_b", 1024)):
        seg[name] = w[:, o:o + width]
        o += width
    z = lambda n: jnp.zeros((D, n), w.dtype)
    small = jnp.concatenate([seg["ga"], seg["iw"], z(LANES - GLA_GATE_RANK - IDX_HEADS)], axis=1)
    parts = [seg["dq"], seg["gate_a"], seg["gate_b"], seg["gv"], seg["gr"], seg["gq"], seg["gk"], seg["iq"],
             seg["ckv"], jnp.concatenate([seg["ik"], z(LANES - IDX_DIM)], axis=1), small]
    used = sum(p.shape[1] for p in parts)
    parts.append(z(PROJ_COLS - used))
    return jnp.concatenate(parts, axis=1).astype(BF16)


def _pair_block_diag(w_uv):
    H, dc, dh = w_uv.shape
    z = jnp.zeros((dc, dh), w_uv.dtype)
    blocks = [jnp.concatenate([jnp.concatenate([w_uv[2 * j], z], axis=1),
                               jnp.concatenate([z, w_uv[2 * j + 1]], axis=1)], axis=0)
              for j in range(H // 2)]
    return jnp.stack(blocks).astype(BF16)


def kernel(x, p, w_in, w_gla_gate_up, b_gla_gate, g_gla_norm, w_gla_proj, g_ckv_norm, w_uv, w_dsa_proj,
           w_out, ln1_g, ln1_b, w_up, conv_w, conv_b, w_down, ln2_g, ln2_b, w_ple, w_ple_gate, ln3_g,
           ln3_b):
    B, L, D = x.shape
    depth = w_in.shape[0]
    alpha = (2.0 * depth) ** 0.25
    T = B * L
    x2d = x.reshape(T, D)
    row = lambda v: v.reshape(1, -1)
    for i in range(depth):
        w_pad = _permute_w_in(w_in[i])
        proj = _proj(x2d, w_pad)
        wup_pad = jnp.concatenate(
            [w_gla_gate_up[i], jnp.zeros((LANES - GLA_GATE_RANK, GLA_HEADS * GLA_DK), F32)], axis=0).astype(BF16)
        ya = _gla(proj, B, L, wup_pad, row(b_gla_gate[i]), row(g_gla_norm[i]), w_gla_proj[i].astype(BF16))
        ob = _dsa(proj, B, L, row(g_ckv_norm[i]))
        x1 = _merge(ya, ob, proj, x2d, _pair_block_diag(w_uv[i]), w_dsa_proj[i].astype(BF16),
                    w_out[i].astype(BF16), row(ln1_g[i]), row(ln1_b[i]), alpha)
        x2d = _ffn(x1, L, w_up[i].astype(BF16), conv_w[i], row(conv_b[i]), w_down[i].astype(BF16),
                   row(ln2_g[i]), row(ln2_b[i]), p[i].reshape(T, PLE_DIM), w_ple[i].astype(BF16),
                   w_ple_gate[i].astype(BF16), row(ln3_g[i]), row(ln3_b[i]), alpha)
    return x2d.reshape(B, L, D)
```

```python
import functools

import jax
import jax.numpy as jnp
from jax import lax
from jax.experimental import pallas as pl
from jax.experimental.pallas import tpu as pltpu

F32 = jnp.float32
BF16 = jnp.bfloat16
I32 = jnp.int32

D_MODEL = 1024
GLA_HEADS = 4
GLA_DK = 128
GLA_DV = 256
GLA_GATE_RANK = 16
GLA_GATE_TEMP = 16.0
GLA_CHUNK = 64
DSA_HEADS = 16
DSA_HEAD_DIM = 64
DSA_LATENT = 128
IDX_HEADS = 8
IDX_DIM = 64
TOPK_MAX = 256
D_FF = 2816
CONV_W = 3
LN_EPS = 1e-5
PLE_DIM = 256

LANES = 128
SUBLANES = 8
INT_MIN = -2147483648
MASK_LOW31 = 0x7FFFFFFF
MASKED_LOGIT = -1e30
MAX_SAFE_SHIFT = 40.0
SHIFT_SLACK = 1.02
LOG2_E = 1.4426950408889634

PROJ_COLS = 8192
COL_DQ = 0
COL_GATE_A = 2048
COL_GATE_B = 3072
COL_GV = 4096
COL_GR = 5120
COL_GQ = 6144
COL_GK = 6656
COL_IQ = 7168
COL_CKV = 7680
COL_IK = 7808
COL_SMALL = 7936
SMALL_IW = GLA_GATE_RANK

MIB = 1024 * 1024
PROJ_TM, PROJ_TN = 2048, 1024
GLA_TT = 256
DSA_TQ, DSA_TK = 128, 512
ROW_TM = 512
VMEM_SMALL = 40 * MIB
VMEM_FFN = 52 * MIB
VMEM_DSA = 56 * MIB

NT_DIMS = (((1,), (1,)), ((), ()))
TN_DIMS = (((0,), (0,)), ((), ()))


def _layer_norm(v, g, b):
    mu = jnp.mean(v, axis=-1, keepdims=True)
    d = v - mu
    var = jnp.mean(d * d, axis=-1, keepdims=True)
    return d * lax.rsqrt(var + LN_EPS) * g + b


def _sigmoid(v):
    return 1.0 / (1.0 + jnp.exp(-v))


def _proj_body(x_ref, w_ref, o_ref):
    o_ref[...] = jnp.dot(x_ref[...].astype(BF16), w_ref[...], preferred_element_type=F32)


def _proj(x2d, w_pad):
    T, D = x2d.shape
    N = w_pad.shape[1]
    tm = min(PROJ_TM, T)
    tn = PROJ_TN
    return pl.pallas_call(
        _proj_body,
        grid=(T // tm, N // tn),
        in_specs=[pl.BlockSpec((tm, D), lambda i, j: (i, 0)),
                  pl.BlockSpec((D, tn), lambda i, j: (0, j))],
        out_specs=pl.BlockSpec((tm, tn), lambda i, j: (i, j)),
        out_shape=jax.ShapeDtypeStruct((T, N), F32),
        compiler_params=pltpu.CompilerParams(
            dimension_semantics=("parallel", "arbitrary"),
            vmem_limit_bytes=VMEM_FFN),
        name="proj",
    )(x2d, w_pad)


def _gla_body(gq_ref, gk_ref, gv_ref, gr_ref, small_ref, gate_ref, wup_ref, bup_ref, gnorm_ref,
              wproj_ref, out_ref, st_ref, oa_ref, *, tt):
    @pl.when(pl.program_id(1) == 0)
    def _():
        st_ref[...] = jnp.zeros(st_ref.shape, F32)

    C = GLA_CHUNK
    nc = tt // C
    row = lax.broadcasted_iota(I32, (tt, tt), 0)
    col = lax.broadcasted_iota(I32, (tt, tt), 1)
    tril = jnp.logical_and(row >= col, row // C == col // C)
    tril_b = jnp.where(tril, 1.0, 0.0).astype(BF16)
    z = jnp.dot(small_ref[...].astype(BF16), wup_ref[...], preferred_element_type=F32) + bup_ref[...]
    log_a = (jnp.minimum(z, 0.0) - jnp.log(1.0 + jnp.exp(-jnp.abs(z)))) * (1.0 / GLA_GATE_TEMP)
    la_hi = log_a.astype(BF16)
    r1 = log_a - la_hi.astype(F32)
    la_mid = r1.astype(BF16)
    la_lo = (r1 - la_mid.astype(F32)).astype(BF16)
    bc = (jnp.dot(tril_b, la_hi, preferred_element_type=F32)
          + jnp.dot(tril_b, la_mid, preferred_element_type=F32)
          + jnp.dot(tril_b, la_lo, preferred_element_type=F32))
    bl = jnp.concatenate(
        [jnp.broadcast_to(bc[(c + 1) * C - 1:(c + 1) * C, :], (C, bc.shape[1])) for c in range(nc)], axis=0)
    e_b = jnp.exp(bc)
    e_nb = jnp.exp(-bc)
    e_bl = jnp.exp(bl - bc)
    for h in range(GLA_HEADS):
        ks = slice(h * GLA_DK, (h + 1) * GLA_DK)
        vs = slice(h * GLA_DV, (h + 1) * GLA_DV)
        q_in = (gq_ref[:, ks] * (GLA_DK ** -0.5) * e_b[:, ks]).astype(BF16)
        k = gk_ref[:, ks]
        k_in = (k * e_nb[:, ks]).astype(BF16)
        k_st = (k * e_bl[:, ks]).astype(BF16)
        v = gv_ref[:, vs].astype(BF16)
        att = lax.dot_general(q_in, k_in, NT_DIMS, preferred_element_type=F32)
        att = jnp.where(tril, att, 0.0).astype(BF16)
        o_intra = jnp.dot(att, v, preferred_element_type=F32)
        st = st_ref[h]
        o_inter = []
        for c in range(nc):
            r = slice(c * C, (c + 1) * C)
            o_inter.append(lax.dot_general(q_in[r], st.astype(BF16), NT_DIMS, preferred_element_type=F32))
            dec = jnp.exp(bc[(c + 1) * C - 1:(c + 1) * C, ks])
            st = st * dec + lax.dot_general(v[r], k_st[r], TN_DIMS, preferred_element_type=F32)
        st_ref[h] = st
        o = o_intra + jnp.concatenate(o_inter, axis=0)
        ms = jnp.mean(o * o, axis=-1, keepdims=True)
        on = o * lax.rsqrt(ms + LN_EPS) * gnorm_ref[:, vs]
        g = gr_ref[:, vs]
        oa_ref[:, vs] = (on * (g * _sigmoid(g))).astype(BF16)
    y = jnp.dot(oa_ref[...], wproj_ref[...], preferred_element_type=F32)
    out_ref[...] = _sigmoid(gate_ref[...]) * y


def _gla(proj, B, L, wup_pad, bup, gnorm, wproj):
    T = B * L
    tt = GLA_TT
    nt = L // tt

    def blk(width, col):
        cb = col // width
        return pl.BlockSpec((tt, width), lambda b, i: (b * nt + i, cb))

    def full(shape):
        return pl.BlockSpec(shape, lambda b, i: (0,) * len(shape))

    return pl.pallas_call(
        functools.partial(_gla_body, tt=tt),
        grid=(B, nt),
        in_specs=[blk(512, COL_GQ), blk(512, COL_GK), blk(1024, COL_GV), blk(1024, COL_GR),
                  blk(128, COL_SMALL), blk(1024, COL_GATE_A),
                  full(wup_pad.shape), full(bup.shape), full(gnorm.shape), full(wproj.shape)],
        out_specs=pl.BlockSpec((tt, D_MODEL), lambda b, i: (b * nt + i, 0)),
        out_shape=jax.ShapeDtypeStruct((T, D_MODEL), F32),
        scratch_shapes=[pltpu.VMEM((GLA_HEADS, GLA_DV, GLA_DK), F32),
                        pltpu.VMEM((tt, GLA_HEADS * GLA_DV), BF16)],
        compiler_params=pltpu.CompilerParams(
            dimension_semantics=("parallel", "arbitrary"),
            vmem_limit_bytes=VMEM_SMALL),
        name="gla",
    )(proj, proj, proj, proj, proj, proj, wup_pad, bup, gnorm, wproj)


SEARCH_BITS = 32
VISITS_PER_TILE = 24
ATT_CHUNKS = 8
TILE_UNROLL = 8


def _for_tiles(n, body, state, unroll):
    def group(width):
        def run(first, st):
            for u in range(width):
                st = body(first + u, st)
            return st
        return run

    state = lax.fori_loop(0, n // unroll, lambda j, st: group(unroll)(unroll * j, st), state)
    done = (n // unroll) * unroll
    width = unroll // 2
    while width >= 1:
        take = (n - done) >= width
        state = lax.cond(take, lambda st, w=width, d=done: group(w)(d, st), lambda st: st, state)
        done = done + jnp.where(take, width, 0)
        width //= 2
    return state


def _dsa_body(dq_ref, iq_ref, ckv_ref, ik_ref, small_ref, gck_ref,
              out_ref, kv_ref, ikk_ref, keys_ref, q2_ref, iq2t_ref, sel_ref, acc_ref, m_ref,
              *, L, tq, tk, topk):
    i = pl.program_id(1)
    nq = L // tq
    nt = L // tk
    dl = DSA_LATENT
    has_b = i < nq
    slot_b = i % 2
    slot_a = 1 - slot_b
    qb = jnp.minimum(i, nq - 1)
    q0 = pl.multiple_of(qb * tq, tq)
    n_b = q0 // tk + 1
    n_a = jnp.where(i >= 1, ((i - 1) * tq) // tk + 1, 0)

    kidx0 = lax.broadcasted_iota(I32, (tk, tq), 0)
    qpos = q0 + lax.broadcasted_iota(I32, (tk, tq), 1)

    def tile_b(v):
        return jnp.broadcast_to(v, (tk, tq))

    @pl.when(i == 0)
    def _():
        kv_ref[...] = jnp.zeros(kv_ref.shape, BF16)
        ikk_ref[...] = jnp.zeros(ikk_ref.shape, BF16)
        keys_ref[...] = jnp.full(keys_ref.shape, INT_MIN, I32)
        sel_ref[...] = jnp.zeros(sel_ref.shape, F32)
        onehot = (lax.broadcasted_iota(I32, (tq, tq), 0) == lax.broadcasted_iota(I32, (tq, tq), 1))
        for s in range(2):
            for h in range(DSA_HEADS):
                q2_ref[s, h * tq:(h + 1) * tq, dl:2 * dl] = jnp.where(onehot, 1.0, 0.0).astype(BF16)

    @pl.when(has_b)
    def _():
        c = ckv_ref[...]
        cn = c * lax.rsqrt(jnp.mean(c * c, axis=-1, keepdims=True) + LN_EPS) * gck_ref[...]
        kv_ref[pl.ds(q0, tq), 0:dl] = cn.astype(BF16)
        kv_ref[pl.ds(q0, tq), dl:2 * dl] = jnp.ones((tq, dl), BF16)
        ikk_ref[pl.ds(q0, tq), :] = ik_ref[...].astype(BF16)

        qn2 = jnp.zeros((tq, 1), F32)
        for h in range(DSA_HEADS):
            qh = dq_ref[:, h * dl:(h + 1) * dl] * (dl ** -0.5)
            q2_ref[slot_b, h * tq:(h + 1) * tq, 0:dl] = (qh * LOG2_E).astype(BF16)
            qn2 = jnp.maximum(qn2, jnp.sum(qh * qh, axis=1, keepdims=True))
        k_bound = (dl ** 0.5) * SHIFT_SLACK * jnp.max(jnp.abs(gck_ref[...]))
        qn2_row = jnp.broadcast_to(qn2, (tq, LANES)).T[0:1, :]
        sel_ref[slot_b, 2:3, :] = jnp.sqrt(qn2_row) * k_bound

        for h in range(IDX_HEADS):
            iqh = jnp.concatenate(
                [iq_ref[:, h * IDX_DIM:(h + 1) * IDX_DIM], jnp.zeros((tq, LANES - IDX_DIM), F32)], axis=1)
            iq2t_ref[:, h * tq:(h + 1) * tq] = iqh.T.astype(BF16)
        sm_t = small_ref[...].T
        iw_scale = (IDX_HEADS ** -0.5) * (IDX_DIM ** -0.5)
        w_rows = [sm_t[SMALL_IW + h:SMALL_IW + h + 1, :] * iw_scale for h in range(IDX_HEADS)]

        def score_keys(kt):
            k0 = pl.multiple_of(kt * tk, tk)
            s_all = jnp.dot(ikk_ref[pl.ds(k0, tk), :], iq2t_ref[...], preferred_element_type=F32)
            sc = jnp.zeros((tk, tq), F32)
            for h in range(IDX_HEADS):
                sc = sc + w_rows[h] * jnp.maximum(s_all[:, h * tq:(h + 1) * tq], 0.0)
            bits = lax.bitcast_convert_type(sc, I32)
            key = jnp.where(bits < 0, bits ^ MASK_LOW31, bits)
            keys_ref[slot_b * nt + kt] = jnp.where(k0 + kidx0 <= qpos, key, INT_MIN)

        def score_tile(kt, carry):
            score_keys(kt)
            return carry

        _for_tiles(n_b, score_tile, 0, TILE_UNROLL)

    n_acc = 4
    acc_rows = n_acc * SUBLANES

    def b_keys(kt):
        return keys_ref[slot_b * nt + kt]

    def partial_count(hit):
        return jnp.sum(hit.reshape(tk // acc_rows, acc_rows, tq), axis=0)

    def visit(state):
        thr, cnt_acc, open_q, p, kt = state
        cand = thr + lax.shift_left(jnp.int32(1), jnp.maximum(SEARCH_BITS - 1 - p, 0))
        cnt_acc = cnt_acc + partial_count(jnp.where(b_keys(kt) >= tile_b(cand), 1.0, 0.0))
        last = kt == n_b - 1
        cnt = jnp.sum(cnt_acc, axis=0, keepdims=True)
        take = jnp.logical_and(jnp.logical_and(last, p < SEARCH_BITS), cnt >= topk)
        thr = jnp.where(take, cand, thr)
        open_q = jnp.where(jnp.logical_and(take, cnt == topk), 0.0, open_q)
        cnt_acc = jnp.where(last, 0.0, cnt_acc)
        return thr, cnt_acc, open_q, jnp.where(last, p + 1, p), jnp.where(last, 0, kt + 1)

    open0 = jnp.where(qpos[0:1, :] + 1 >= topk, 1.0, 0.0)
    search0 = (jnp.full((1, tq), INT_MIN, I32), jnp.zeros((acc_rows, tq), F32), open0,
               jnp.int32(0), jnp.int32(0))

    thr_a = tile_b(lax.bitcast_convert_type(sel_ref[slot_a, 0:1, :], I32))
    cut_a = tile_b(sel_ref[slot_a, 1:2, :].astype(I32))
    shift_a = sel_ref[slot_a, 2:3, :]
    acc_ref[...] = jnp.zeros(acc_ref.shape, F32)

    def selected_a(kt):
        key = keys_ref[slot_a * nt + kt]
        return jnp.logical_or(key > thr_a, jnp.logical_and(key == thr_a, kt * tk + kidx0 <= cut_a))

    def att_bounded(state):
        neg_shift = tile_b(-shift_a * LOG2_E)

        def tile(kt, st):
            k0 = pl.multiple_of(kt * tk, tk)
            bias_t = jnp.where(selected_a(kt), neg_shift, MASKED_LOGIT).astype(BF16)
            rhs = jnp.concatenate([kv_ref[pl.ds(k0, tk), 0:dl], bias_t], axis=1)
            kvt = kv_ref[pl.ds(k0, tk), :]
            rows = DSA_HEADS * tq // ATT_CHUNKS
            for c in range(ATT_CHUNKS):
                rs = slice(c * rows, (c + 1) * rows)
                logits = lax.dot_general(q2_ref[slot_a, rs, :], rhs, NT_DIMS, preferred_element_type=F32)
                p = jnp.exp2(logits).astype(BF16)
                acc_ref[rs, :] += jnp.dot(p, kvt, preferred_element_type=F32)
                for _ in range(VISITS_PER_TILE // ATT_CHUNKS):
                    st = visit(st)
            return st

        return _for_tiles(n_a, tile, state, TILE_UNROLL)

    def att_online(state):
        m_ref[...] = jnp.full(m_ref.shape, MASKED_LOGIT, F32)

        def tile(kt, carry):
            k0 = pl.multiple_of(kt * tk, tk)
            kvt = kv_ref[pl.ds(k0, tk), :]
            bias = jnp.where(selected_a(kt), 0.0, -jnp.inf).T
            logits = lax.dot_general(q2_ref[slot_a, :, 0:dl], kvt[:, 0:dl], NT_DIMS,
                                     preferred_element_type=F32) * (1.0 / LOG2_E)
            logits = (logits.reshape(DSA_HEADS, tq, tk) + bias[None]).reshape(DSA_HEADS * tq, tk)
            m_old = m_ref[...]
            m_new = jnp.maximum(m_old, jnp.max(logits, axis=1, keepdims=True))
            alpha = jnp.exp(m_old - m_new)
            p = jnp.exp(logits - m_new).astype(BF16)
            acc_ref[...] = acc_ref[...] * alpha + jnp.dot(p, kvt, preferred_element_type=F32)
            m_ref[...] = m_new
            return carry

        lax.fori_loop(0, n_a, tile, 0)
        return state

    state = lax.cond(jnp.max(shift_a) <= MAX_SAFE_SHIFT, att_bounded, att_online, search0)

    def in_pass(st):
        return jnp.logical_and(jnp.logical_and(st[4] > 0, st[3] < SEARCH_BITS), has_b)

    state = lax.while_loop(in_pass, visit, state)

    def passes_left(st):
        return jnp.logical_and(jnp.logical_and(st[3] < SEARCH_BITS, has_b), jnp.max(st[2]) > 0.0)

    def one_pass(st):
        thr, cnt_acc, open_q, p, kt = st
        cand = thr + lax.shift_left(jnp.int32(1), SEARCH_BITS - 1 - p)
        cand_t = tile_b(cand)
        hits = _for_tiles(
            n_b, lambda t, a: a + partial_count(jnp.where(b_keys(t) >= cand_t, 1.0, 0.0)), cnt_acc,
            TILE_UNROLL)
        cnt = jnp.sum(hits, axis=0, keepdims=True)
        take = cnt >= topk
        thr = jnp.where(take, cand, thr)
        open_q = jnp.where(jnp.logical_and(take, cnt == topk), 0.0, open_q)
        return thr, cnt_acc, open_q, p + 1, kt

    thr = lax.while_loop(passes_left, one_pass, state)[0]
    thr_t = tile_b(thr)

    def count(pred):
        def body(kt, acc):
            return acc + partial_count(jnp.where(pred(b_keys(kt), kt * tk + kidx0), 1.0, 0.0))
        acc = lax.fori_loop(0, jnp.where(has_b, n_b, 0), body, jnp.zeros((acc_rows, tq), F32))
        return jnp.sum(acc, axis=0, keepdims=True)

    n_gt = count(lambda k, s: k > thr_t)
    n_ge = count(lambda k, s: k >= thr_t)
    need = topk - n_gt
    is_min = thr == INT_MIN
    partial = jnp.logical_and(n_ge - n_gt > need, jnp.logical_not(is_min))
    any_partial = jnp.max(jnp.where(partial, 1.0, 0.0)) > 0.0
    nbits = max(1, (L - 1).bit_length())

    def tie_search():
        def idx_pass(p, d):
            cand = d + lax.shift_left(jnp.int32(1), nbits - 1 - p)
            cb = tile_b(cand)
            cnt = count(lambda k, s: jnp.logical_and(k == thr_t, s < cb))
            return jnp.where(cnt < need, cand, d)
        return lax.fori_loop(0, nbits, idx_pass, jnp.zeros((1, tq), I32))

    cut = lax.cond(any_partial, tie_search, lambda: jnp.zeros((1, tq), I32))
    cut = jnp.where(partial, cut, L)
    cut = jnp.where(is_min, -1, cut)
    sel_ref[slot_b, 0:1, :] = lax.bitcast_convert_type(thr, F32)
    sel_ref[slot_b, 1:2, :] = cut.astype(F32)

    @pl.when(i >= 1)
    def _():
        acc = acc_ref[...]
        o = acc[:, 0:dl] * (1.0 / acc[:, dl:2 * dl])
        for h in range(DSA_HEADS):
            out_ref[:, h * dl:(h + 1) * dl] = o[h * tq:(h + 1) * tq, :].astype(BF16)


def _dsa(proj, B, L, gck):
    T = B * L
    tq = DSA_TQ
    tk = min(DSA_TK, L)
    nq = L // tq
    topk = min(TOPK_MAX, L // 4)
    assert L % tk == 0 and tk >= topk and tq == LANES

    def blk(width, col):
        cb = col // width
        return pl.BlockSpec((tq, width), lambda b, i: (b * nq + jnp.minimum(i, nq - 1), cb))

    def full(shape):
        return pl.BlockSpec(shape, lambda b, i: (0,) * len(shape))

    return pl.pallas_call(
        functools.partial(_dsa_body, L=L, tq=tq, tk=tk, topk=topk),
        grid=(B, nq + 1),
        in_specs=[blk(2048, COL_DQ), blk(512, COL_IQ), blk(128, COL_CKV), blk(128, COL_IK),
                  blk(128, COL_SMALL), full(gck.shape)],
        out_specs=pl.BlockSpec((tq, DSA_HEADS * DSA_LATENT), lambda b, i: (b * nq + jnp.maximum(i - 1, 0), 0)),
        out_shape=jax.ShapeDtypeStruct((T, DSA_HEADS * DSA_LATENT), BF16),
        scratch_shapes=[pltpu.VMEM((L, 2 * DSA_LATENT), BF16),
                        pltpu.VMEM((L, LANES), BF16),
                        pltpu.VMEM((2 * (L // tk), tk, tq), I32),
                        pltpu.VMEM((2, DSA_HEADS * tq, 2 * DSA_LATENT), BF16),
                        pltpu.VMEM((LANES, IDX_HEADS * tq), BF16),
                        pltpu.VMEM((2, SUBLANES, tq), F32),
                        pltpu.VMEM((DSA_HEADS * tq, 2 * DSA_LATENT), F32),
                        pltpu.VMEM((DSA_HEADS * tq, 1), F32)],
        compiler_params=pltpu.CompilerParams(
            dimension_semantics=("parallel", "arbitrary"),
            vmem_limit_bytes=VMEM_DSA),
        name="dsa",
    )(proj, proj, proj, proj, proj, gck)


def _merge_body(a_ref, ob_ref, gate_ref, x_ref, wuv_ref, wdsa_ref, w_ref, g_ref, be_ref, o_ref, *, alpha):
    ob = ob_ref[...]
    pw = 2 * DSA_LATENT
    u = jnp.concatenate(
        [jnp.dot(ob[:, j * pw:(j + 1) * pw], wuv_ref[j], preferred_element_type=F32).astype(BF16)
         for j in range(DSA_HEADS // 2)], axis=1)
    yb = jnp.dot(u, wdsa_ref[...], preferred_element_type=F32)
    s = (a_ref[...] + _sigmoid(gate_ref[...]) * yb).astype(BF16)
    mixed = jnp.dot(s, w_ref[...], preferred_element_type=F32)
    o_ref[...] = _layer_norm(alpha * x_ref[...] + mixed, g_ref[...], be_ref[...])


def _merge(ya, ob, proj, x2d, wuv_pairs, wdsa, w_out, g, b, alpha):
    T = x2d.shape[0]
    tm = min(ROW_TM, T)
    row = pl.BlockSpec((tm, D_MODEL), lambda i: (i, 0))
    gate_cb = COL_GATE_B // D_MODEL

    def full(shape):
        return pl.BlockSpec(shape, lambda i: (0,) * len(shape))

    return pl.pallas_call(
        functools.partial(_merge_body, alpha=alpha),
        grid=(T // tm,),
        in_specs=[row, pl.BlockSpec((tm, DSA_HEADS * DSA_LATENT), lambda i: (i, 0)),
                  pl.BlockSpec((tm, D_MODEL), lambda i: (i, gate_cb)), row,
                  full(wuv_pairs.shape), full(wdsa.shape), full(w_out.shape), full(g.shape), full(b.shape)],
        out_specs=row,
        out_shape=jax.ShapeDtypeStruct((T, D_MODEL), F32),
        compiler_params=pltpu.CompilerParams(
            dimension_semantics=("parallel",), vmem_limit_bytes=VMEM_SMALL),
        name="merge",
    )(ya, ob, proj, x2d, wuv_pairs, wdsa, w_out, g, b)


FFN_HALO = 16
FFN_CHUNK = 256


def _ffn_body(xm_ref, xh_ref, wup_ref, cw_ref, cb_ref, wd_ref, ln2g_ref, ln2b_ref, p_ref, wple_ref,
              wpg_ref, ln3g_ref, ln3b_ref, o_ref, act_ref, h_ref, *, tm, L, alpha):
    i = pl.program_id(0)
    at_start = (i * tm) % L == 0
    halo = jnp.where(at_start, 0.0, xh_ref[...])
    xb = jnp.concatenate([halo, xm_ref[...]], axis=0).astype(BF16)

    def conv_branch(col, slot):
        cs = slice(col, col + FFN_CHUNK)
        h_ref[slot] = jnp.dot(xb, wup_ref[:, cs], preferred_element_type=F32)
        hs = h_ref.at[slot]
        return (cw_ref[2:3, cs] * hs[FFN_HALO:FFN_HALO + tm, :]
                + cw_ref[0:1, cs] * hs[FFN_HALO - 2:FFN_HALO - 2 + tm, :]
                + cw_ref[1:2, cs] * hs[FFN_HALO - 1:FFN_HALO - 1 + tm, :]
                + cb_ref[:, cs])

    for c in range(D_FF // FFN_CHUNK):
        hg = conv_branch(c * FFN_CHUNK, (2 * c) % 4)
        hv = conv_branch(D_FF + c * FFN_CHUNK, (2 * c + 1) % 4)
        act_ref[:, c * FFN_CHUNK:(c + 1) * FFN_CHUNK] = ((hg * _sigmoid(hg)) * hv).astype(BF16)

    ffn = jnp.dot(act_ref[...], wd_ref[...], preferred_element_type=F32)
    x2 = _layer_norm(alpha * xm_ref[...] + ffn, ln2g_ref[...], ln2b_ref[...])
    gate = _sigmoid(jnp.dot(x2.astype(BF16), wpg_ref[...], preferred_element_type=F32))
    ple = gate * jnp.dot(p_ref[...].astype(BF16), wple_ref[...], preferred_element_type=F32)
    o_ref[...] = _layer_norm(alpha * x2 + ple, ln3g_ref[...], ln3b_ref[...])


def _ffn(x1, L, w_up, conv_w, conv_b, w_down, ln2g, ln2b, p2d, w_ple, w_pg, ln3g, ln3b, alpha):
    T = x1.shape[0]
    tm = min(ROW_TM, L)
    hb = tm // FFN_HALO

    def full(shape):
        return pl.BlockSpec(shape, lambda i: (0,) * len(shape), pipeline_mode=pl.Buffered(1))

    return pl.pallas_call(
        functools.partial(_ffn_body, tm=tm, L=L, alpha=alpha),
        grid=(T // tm,),
        in_specs=[pl.BlockSpec((tm, D_MODEL), lambda i: (i, 0)),
                  pl.BlockSpec((FFN_HALO, D_MODEL), lambda i: (jnp.maximum(i * hb - 1, 0), 0)),
                  full(w_up.shape), full(conv_w.shape), full(conv_b.shape), full(w_down.shape),
                  full(ln2g.shape), full(ln2b.shape),
                  pl.BlockSpec((tm, PLE_DIM), lambda i: (i, 0)),
                  full(w_ple.shape), full(w_pg.shape), full(ln3g.shape), full(ln3b.shape)],
        out_specs=pl.BlockSpec((tm, D_MODEL), lambda i: (i, 0)),
        out_shape=jax.ShapeDtypeStruct((T, D_MODEL), F32),
        scratch_shapes=[pltpu.VMEM((tm, D_FF), BF16),
                        pltpu.VMEM((4, tm + FFN_HALO, FFN_CHUNK), F32)],
        compiler_params=pltpu.CompilerParams(
            dimension_semantics=("parallel",),
            vmem_limit_bytes=VMEM_FFN),
        name="ffn",
    )(x1, x1, w_up, conv_w, conv_b, w_down, ln2g, ln2b, p2d, w_ple, w_pg, ln3g, ln3b)


def _permute_w_in(w):
    D = w.shape[0]
    o = 0
    seg = {}
    for name, width in (("gq", 512), ("gk", 512), ("gv", 1024), ("gr", 1024), ("ga", GLA_GATE_RANK),
                        ("dq", 2048), ("ckv", 128), ("iq", 512), ("ik", 64), ("iw", 8),
                        ("gate_a", 1024), ("gate_b", 1024)):
        seg[name] = w[:, o:o + width]
        o += width
    z = lambda n: jnp.zeros((D, n), w.dtype)
    small = jnp.concatenate([seg["ga"], seg["iw"], z(LANES - GLA_GATE_RANK - IDX_HEADS)], axis=1)
    parts = [seg["dq"], seg["gate_a"], seg["gate_b"], seg["gv"], seg["gr"], seg["gq"], seg["gk"], seg["iq"],
             seg["ckv"], jnp.concatenate([seg["ik"], z(LANES - IDX_DIM)], axis=1), small]
    used = sum(p.shape[1] for p in parts)
    parts.append(z(PROJ_COLS - used))
    return jnp.concatenate(parts, axis=1).astype(BF16)


def _pair_block_diag(w_uv):
    H, dc, dh = w_uv.shape
    z = jnp.zeros((dc, dh), w_uv.dtype)
    blocks = [jnp.concatenate([jnp.concatenate([w_uv[2 * j], z], axis=1),
                               jnp.concatenate([z, w_uv[2 * j + 1]], axis=1)], axis=0)
              for j in range(H // 2)]
    return jnp.stack(blocks).astype(BF16)


def kernel(x, p, w_in, w_gla_gate_up, b_gla_gate, g_gla_norm, w_gla_proj, g_ckv_norm, w_uv, w_dsa_proj,
           w_out, ln1_g, ln1_b, w_up, conv_w, conv_b, w_down, ln2_g, ln2_b, w_ple, w_ple_gate, ln3_g,
           ln3_b):
    B, L, D = x.shape
    depth = w_in.shape[0]
    alpha = (2.0 * depth) ** 0.25
    T = B * L
    x2d = x.reshape(T, D)
    row = lambda v: v.reshape(1, -1)
    for i in range(depth):
        w_pad = _permute_w_in(w_in[i])
        proj = _proj(x2d, w_pad)
        wup_pad = jnp.concatenate(
            [w_gla_gate_up[i], jnp.zeros((LANES - GLA_GATE_RANK, GLA_HEADS * GLA_DK), F32)], axis=0).astype(BF16)
        ya = _gla(proj, B, L, wup_pad, row(b_gla_gate[i]), row(g_gla_norm[i]), w_gla_proj[i].astype(BF16))
        ob = _dsa(proj, B, L, row(g_ckv_norm[i]))
        x1 = _merge(ya, ob, proj, x2d, _pair_block_diag(w_uv[i]), w_dsa_proj[i].astype(BF16),
                    w_out[i].astype(BF16), row(ln1_g[i]), row(ln1_b[i]), alpha)
        x2d = _ffn(x1, L, w_up[i].astype(BF16), conv_w[i], row(conv_b[i]), w_down[i].astype(BF16),
                   row(ln2_g[i]), row(ln2_b[i]), p[i].reshape(T, PLE_DIM), w_ple[i].astype(BF16),
                   w_ple_gate[i].astype(BF16), row(ln3_g[i]), row(ln3_b[i]), alpha)
    return x2d.reshape(B, L, D)
```
